```python
import math
import jax
import jax.numpy as jnp
from jax import lax
import numpy as np

D_MODEL = 4096
BATCH = 2
SEQ = 4096
DEPTH = 4

CTX_LEN = 256
GRID_W = 64
N_MIXERS = 3
D_FF = 4 * D_MODEL
ADA_RANK = 512
N_MOD = 6
ROPE_BASE = 10000.0
LN_EPS = 1e-5
RMS_EPS = 1e-6
Q_BLOCK = 128
DN_ALPHA = (2.0 * DEPTH) ** 0.25
DN_BETA = (8.0 * DEPTH) ** -0.25

DA_QK_DIM = 64
DA_V_DIM = 2 * DA_QK_DIM
DA_HEADS = D_MODEL // DA_V_DIM
MLA_HEADS = D_MODEL // 64
MLA_Q_RANK = 1536
MLA_KV_RANK = 512
MLA_NOPE = 128
MLA_ROPE = 64
MLA_V = 128
NA_HEAD_DIM = 128
NA_HEADS = D_MODEL // NA_HEAD_DIM
NA_WIN_ROWS = 8
NA_WIN_COLS = 16

N_DA = len(range(0, DEPTH, N_MIXERS))
N_MLA = len(range(1, DEPTH, N_MIXERS))
N_NA = len(range(2, DEPTH, N_MIXERS))

kernel_name = 'hybrid_diffattn_mla_natten_flow_backbone'


def layer_norm(x, g, b):
    xf = x.astype(jnp.float32)
    mu = jnp.mean(xf, -1, keepdims=True)
    var = jnp.mean(jnp.square(xf - mu), -1, keepdims=True)
    return ((xf - mu) * lax.rsqrt(var + LN_EPS)).astype(x.dtype) * g + b


def rms_norm(x, g, eps=RMS_EPS):
    xf = x.astype(jnp.float32)
    return (xf * lax.rsqrt(jnp.mean(jnp.square(xf), -1, keepdims=True) + eps)).astype(x.dtype) * g


def axial_rope(x, rows, cols):
    half = x.shape[-1] // 2
    quarter = half // 2
    inv = ROPE_BASE ** (-jnp.arange(quarter, dtype=jnp.float32) / quarter)

    def rot(xh, pos):
        ang = pos.astype(jnp.float32)[:, None] * inv
        cos = jnp.cos(ang)[:, None, :].astype(x.dtype)
        sin = jnp.sin(ang)[:, None, :].astype(x.dtype)
        x1, x2 = xh[..., :quarter], xh[..., quarter:]
        return jnp.concatenate([x1 * cos - x2 * sin, x1 * sin + x2 * cos], -1)

    return jnp.concatenate([rot(x[..., :half], rows), rot(x[..., half:], cols)], -1)


def modulation(cond, w1, w2, b):
    m = (jax.nn.silu(cond) @ w1) @ w2 + b
    return m.reshape(cond.shape[0], N_MOD, 1, D_MODEL)


def query_blocks(q):
    B, S = q.shape[:2]
    return jnp.moveaxis(q.reshape((B, S // Q_BLOCK, Q_BLOCK) + q.shape[2:]), 1, 0)


def merge_blocks(o):
    o = jnp.moveaxis(o, 0, 1)
    return o.reshape((o.shape[0], o.shape[1] * o.shape[2]) + o.shape[3:])


def sq_relu_mlp(h, w1, w2):
    return jnp.square(jax.nn.relu(h @ w1)) @ w2


def diff_attention(hl, hc, w_qkv, w_o, lq1, lk1, lq2, lk2, sub_g, lam_init, rows, cols, with_ctx):
    B, S, _ = hl.shape
    C = hc.shape[1]

    def project(h):
        q, k, v = jnp.split(h @ w_qkv, 3, axis=-1)
        L = h.shape[1]
        return (q.reshape(B, L, DA_HEADS * 2, DA_QK_DIM), k.reshape(B, L, DA_HEADS * 2, DA_QK_DIM),
                v.reshape(B, L, DA_HEADS, DA_V_DIM))

    ql, kl, vl = project(hl)
    qc, kc, vc = project(hc)
    comp = (DA_HEADS, 2, DA_QK_DIM)
    ql = axial_rope(ql, rows, cols).reshape((B, S) + comp)
    kl = axial_rope(kl, rows, cols).reshape((B, S) + comp)
    qc = qc.reshape((B, C) + comp)
    kc = kc.reshape((B, C) + comp)
    lam = (jnp.exp(jnp.sum(lq1 * lk1).astype(jnp.float32))
           - jnp.exp(jnp.sum(lq2 * lk2).astype(jnp.float32)) + lam_init)
    scale = DA_QK_DIM ** -0.5

    def attend(q, k, v):
        s = jnp.einsum('bqhcd,bkhcd->bhcqk', q, k).astype(jnp.float32) * scale
        p = jax.nn.softmax(s, axis=-1)
        a = (p[:, :, 0] - lam * p[:, :, 1]).astype(v.dtype)
        return jnp.einsum('bhqk,bkhd->bqhd', a, v)

    def finish(o):
        o = rms_norm(o, sub_g, eps=1e-5) * (1.0 - lam_init)
        return o.reshape(o.shape[0], o.shape[1], DA_HEADS * DA_V_DIM) @ w_o

    k_all = jnp.concatenate([kc, kl], axis=1)
    v_all = jnp.concatenate([vc, vl], axis=1)
    ol = merge_blocks(lax.map(lambda qb: attend(qb, k_all, v_all), query_blocks(ql)))
    yl = finish(ol)
    yc = finish(attend(qc, kc, vc)) if with_ctx else None
    return yl, yc


def mla_attention(hl, hc, w_down, q_g, w_uq, kv_g, w_ukv, w_o, rows, cols, with_ctx):
    def project(h, use_rope):
        B, L, _ = h.shape
        down = h @ w_down
        cq = rms_norm(down[..., :MLA_Q_RANK], q_g)
        ckv = rms_norm(down[..., MLA_Q_RANK:MLA_Q_RANK + MLA_KV_RANK], kv_g)
        k_rope = down[..., MLA_Q_RANK + MLA_KV_RANK:][:, :, None, :]
        q = (cq @ w_uq).reshape(B, L, MLA_HEADS, MLA_NOPE + MLA_ROPE)
        q_nope, q_rope = q[..., :MLA_NOPE], q[..., MLA_NOPE:]
        kv = (ckv @ w_ukv).reshape(B, L, MLA_HEADS, MLA_NOPE + MLA_V)
        k_nope, v = kv[..., :MLA_NOPE], kv[..., MLA_NOPE:]
        if use_rope:
            q_rope = axial_rope(q_rope, rows, cols)
            k_rope = axial_rope(k_rope, rows, cols)
        return q_nope, q_rope, k_nope, k_rope[:, :, 0], v

    scale = (MLA_NOPE + MLA_ROPE) ** -0.5

    def attend(q_nope, q_rope, k_nope, k_rope, v):
        s = (jnp.einsum('bqhd,bkhd->bhqk', q_nope, k_nope)
             + jnp.einsum('bqhr,bkr->bhqk', q_rope, k_rope)).astype(jnp.float32) * scale
        p = jax.nn.softmax(s, axis=-1).astype(v.dtype)
        return jnp.einsum('bhqk,bkhd->bqhd', p, v)

    def finish(o):
        return o.reshape(o.shape[0], o.shape[1], MLA_HEADS * MLA_V) @ w_o

    qn_l, qr_l, kn_l, kr_l, v_l = project(hl, True)
    qn_c, qr_c, kn_c, kr_c, v_c = project(hc, False)
    kn_all = jnp.concatenate([kn_c, kn_l], axis=1)
    kr_all = jnp.concatenate([kr_c, kr_l], axis=1)
    v_all = jnp.concatenate([v_c, v_l], axis=1)
    ol = merge_blocks(lax.map(lambda qs: attend(qs[0], qs[1], kn_all, kr_all, v_all),
                              (query_blocks(qn_l), query_blocks(qr_l))))
    yl = finish(ol)
    yc = finish(attend(qn_c, qr_c, kn_c, kr_c, v_c)) if with_ctx else None
    return yl, yc


def neighbourhood_attention(hl, hc, w_qkv, w_o, rpb, with_ctx):
    B, S, _ = hl.shape
    C = hc.shape[1]
    n_rows = S // GRID_W
    wr = min(NA_WIN_ROWS, n_rows)

    def project(h):
        q, k, v = jnp.split(h @ w_qkv, 3, axis=-1)
        shp = h.shape[:2] + (NA_HEADS, NA_HEAD_DIM)
        return q.reshape(shp), k.reshape(shp), v.reshape(shp)

    ql, kl, vl = project(hl)
    qc, kc, vc = project(hc)
    grid = (B, n_rows, GRID_W, NA_HEADS, NA_HEAD_DIM)
    qg, kg, vg = ql.reshape(grid), kl.reshape(grid), vl.reshape(grid)
    scale = NA_HEAD_DIM ** -0.5

    col = jnp.arange(GRID_W)
    col_start = jnp.clip(col - NA_WIN_COLS // 2, 0, GRID_W - NA_WIN_COLS)
    col_in = ((col[None, :] >= col_start[:, None])
              & (col[None, :] < col_start[:, None] + NA_WIN_COLS))
    col_idx = jnp.clip(col[None, :] - col[:, None] + NA_WIN_COLS - 1, 0, 2 * NA_WIN_COLS - 2)
    col_bias = rpb[:, :, col_idx].astype(jnp.float32)

    def row_block(r):
        r0 = jnp.clip(r - wr // 2, 0, n_rows - wr)
        q = lax.dynamic_index_in_dim(qg, r, axis=1, keepdims=False)
        kb = lax.dynamic_slice_in_dim(kg, r0, wr, axis=1)
        vb = lax.dynamic_slice_in_dim(vg, r0, wr, axis=1)
        row_idx = r0 + jnp.arange(wr) - r + NA_WIN_ROWS - 1
        bias = jnp.take(col_bias, row_idx, axis=1).transpose(0, 2, 1, 3)
        s_lat = jnp.einsum('bqhd,brkhd->bhqrk', q, kb).astype(jnp.float32) * scale + bias
        s_lat = jnp.where(col_in[:, None, :], s_lat, -jnp.inf)
        s_lat = s_lat.reshape(B, NA_HEADS, GRID_W, wr * GRID_W)
        s_ctx = jnp.einsum('bqhd,bkhd->bhqk', q, kc).astype(jnp.float32) * scale
        p = jax.nn.softmax(jnp.concatenate([s_ctx, s_lat], axis=-1), axis=-1).astype(vb.dtype)
        p_ctx = p[..., :C]
        p_lat = p[..., C:].reshape(B, NA_HEADS, GRID_W, wr, GRID_W)
        return (jnp.einsum('bhqk,bkhd->bqhd', p_ctx, vc)
                + jnp.einsum('bhqrk,brkhd->bqhd', p_lat, vb))

    ol = lax.map(row_block, jnp.arange(n_rows))
    ol = jnp.moveaxis(ol, 0, 1).reshape(B, S, NA_HEADS * NA_HEAD_DIM)
    yl = ol @ w_o
    yc = None
    if with_ctx:
        s = jnp.einsum('bqhd,bkhd->bhqk', qc, kc).astype(jnp.float32) * scale
        p = jax.nn.softmax(s, axis=-1).astype(vc.dtype)
        oc = jnp.einsum('bhqk,bkhd->bqhd', p, vc)
        yc = oc.reshape(B, C, NA_HEADS * NA_HEAD_DIM) @ w_o
    return yl, yc


def setup_inputs(seed: int = 0) -> dict:
    key = jax.random.key(seed)
    kit = iter(list(jax.random.split(key, 40)))

    def nrm(shape, scale=1.0):
        return jax.random.normal(next(kit), shape, jnp.float32) * scale

    D = D_MODEL
    inp = {}
    inp['x'] = nrm((BATCH, SEQ, D))
    inp['c'] = nrm((BATCH, D))
    inp['ctx'] = nrm((BATCH, CTX_LEN, D))
    inp['c_ctx'] = nrm((D,))
    inp['ada_w1'] = nrm((DEPTH, D, ADA_RANK), D ** -0.5)
    inp['ada_w2'] = nrm((DEPTH, ADA_RANK, N_MOD * D), 0.5 * ADA_RANK ** -0.5)
    inp['ada_b'] = nrm((DEPTH, N_MOD * D), 0.02)
    inp['ln_g'] = 1.0 + nrm((DEPTH, 2, D), 0.02)
    inp['ln_b'] = nrm((DEPTH, 2, D), 0.02)
    inp['mlp_w1'] = nrm((DEPTH, D, D_FF), D ** -0.5)
    inp['mlp_w2'] = nrm((DEPTH, D_FF, D), DN_BETA * D_FF ** -0.5)
    inp['da_w_qkv'] = nrm((N_DA, D, 3 * D), D ** -0.5)
    inp['da_w_o'] = nrm((N_DA, DA_HEADS * DA_V_DIM, D), DN_BETA * (DA_HEADS * DA_V_DIM) ** -0.5)
    inp['da_lq1'] = nrm((N_DA, DA_QK_DIM), 0.1)
    inp['da_lk1'] = nrm((N_DA, DA_QK_DIM), 0.1)
    inp['da_lq2'] = nrm((N_DA, DA_QK_DIM), 0.1)
    inp['da_lk2'] = nrm((N_DA, DA_QK_DIM), 0.1)
    inp['da_sub_g'] = 1.0 + nrm((N_DA, DA_V_DIM), 0.02)
    inp['mla_w_down'] = nrm((N_MLA, D, MLA_Q_RANK + MLA_KV_RANK + MLA_ROPE), D ** -0.5)
    inp['mla_q_g'] = 1.0 + nrm((N_MLA, MLA_Q_RANK), 0.02)
    inp['mla_w_uq'] = nrm((N_MLA, MLA_Q_RANK, MLA_HEADS * (MLA_NOPE + MLA_ROPE)), MLA_Q_RANK ** -0.5)
    inp['mla_kv_g'] = 1.0 + nrm((N_MLA, MLA_KV_RANK), 0.02)
    inp['mla_w_ukv'] = nrm((N_MLA, MLA_KV_RANK, MLA_HEADS * (MLA_NOPE + MLA_V)), MLA_KV_RANK ** -0.5)
    inp['mla_w_o'] = nrm((N_MLA, MLA_HEADS * MLA_V, D), DN_BETA * (MLA_HEADS * MLA_V) ** -0.5)
    inp['na_w_qkv'] = nrm((N_NA, D, 3 * NA_HEADS * NA_HEAD_DIM), D ** -0.5)
    inp['na_w_o'] = nrm((N_NA, NA_HEADS * NA_HEAD_DIM, D), DN_BETA * (NA_HEADS * NA_HEAD_DIM) ** -0.5)
    inp['na_rpb'] = nrm((N_NA, NA_HEADS, 2 * NA_WIN_ROWS - 1, 2 * NA_WIN_COLS - 1), 0.1)
    return inp


def reference(x, c, ctx, c_ctx, ada_w1, ada_w2, ada_b, ln_g, ln_b, mlp_w1, mlp_w2,
              da_w_qkv, da_w_o, da_lq1, da_lk1, da_lq2, da_lk2, da_sub_g,
              mla_w_down, mla_q_g, mla_w_uq, mla_kv_g, mla_w_ukv, mla_w_o,
              na_w_qkv, na_w_o, na_rpb):
    S = x.shape[1]
    pos = jnp.arange(S)
    rows = pos // GRID_W
    cols = pos % GRID_W
    xl, xc = x, ctx
    for i in range(DEPTH):
        with_ctx = i < DEPTH - 1
        kind = i % N_MIXERS
        slot = i // N_MIXERS
        ml = modulation(c, ada_w1[i], ada_w2[i], ada_b[i])
        mc = modulation(c_ctx[None], ada_w1[i], ada_w2[i], ada_b[i])
        hl = xl * (1.0 + ml[:, 1]) + ml[:, 0]
        hc = xc * (1.0 + mc[:, 1]) + mc[:, 0]
        if kind == 0:
            lam_init = 0.8 - 0.6 * math.exp(-0.3 * i)
            yl, yc = diff_attention(hl, hc, da_w_qkv[slot], da_w_o[slot], da_lq1[slot], da_lk1[slot],
                                    da_lq2[slot], da_lk2[slot], da_sub_g[slot], lam_init,
                                    rows, cols, with_ctx)
        elif kind == 1:
            yl, yc = mla_attention(hl, hc, mla_w_down[slot], mla_q_g[slot], mla_w_uq[slot],
                                   mla_kv_g[slot], mla_w_ukv[slot], mla_w_o[slot],
                                   rows, cols, with_ctx)
        else:
            yl, yc = neighbourhood_attention(hl, hc, na_w_qkv[slot], na_w_o[slot], na_rpb[slot], with_ctx)
        xl = layer_norm(DN_ALPHA * xl + ml[:, 2] * yl, ln_g[i, 0], ln_b[i, 0])
        hl = xl * (1.0 + ml[:, 4]) + ml[:, 3]
        xl = layer_norm(DN_ALPHA * xl + ml[:, 5] * sq_relu_mlp(hl, mlp_w1[i], mlp_w2[i]),
                        ln_g[i, 1], ln_b[i, 1])
        if with_ctx:
            xc = layer_norm(DN_ALPHA * xc + mc[:, 2] * yc, ln_g[i, 0], ln_b[i, 0])
            hc = xc * (1.0 + mc[:, 4]) + mc[:, 3]
            xc = layer_norm(DN_ALPHA * xc + mc[:, 5] * sq_relu_mlp(hc, mlp_w1[i], mlp_w2[i]),
                            ln_g[i, 1], ln_b[i, 1])
    return xl
```

```python
import functools
import math

import jax
import jax.numpy as jnp
from jax import lax
from jax.experimental import pallas as pl
from jax.experimental.pallas import tpu as pltpu

F32 = jnp.float32
BF16 = jnp.bfloat16

GRID_W = 64
N_MOD = 6
ROPE_BASE = 10000.0
LN_EPS = 1e-5
RMS_EPS = 1e-6
DA_SUB_EPS = 1e-5
DA_QK_DIM = 64
HEAD_V = 128
MLA_NOPE = 128
MLA_ROPE = 64
NA_WIN_ROWS = 8
NA_WIN_COLS = 16
N_MIXERS = 3
MASK_BIAS = -1e30

LANES = 128
VMEM_LIMIT_BYTES = 48 * 1024 * 1024

TQ = 256
NA_TILE_ROWS = TQ // GRID_W
NA_KEY_ROWS = 12
ROW_TILE = 256


def _pick_tile(dim, target, align):
    best = None
    for t in range(align, min(dim, target) + 1, align):
        if dim % t == 0:
            best = t
    return best if best is not None else dim


def _params(*sem):
    return pltpu.CompilerParams(dimension_semantics=sem, vmem_limit_bytes=VMEM_LIMIT_BYTES)


def _rope(x, cos, sin):
    n = x.shape[1]
    reps = n // LANES
    if reps > 1:
        cos = jnp.concatenate([cos] * reps, axis=1)
        sin = jnp.concatenate([sin] * reps, axis=1)
    lane = lax.broadcasted_iota(jnp.int32, x.shape, 1)
    nxt = pltpu.roll(x, n - 16, 1)
    prv = pltpu.roll(x, 16, 1)
    partner = jnp.where((lane & 16) == 0, nxt, prv)
    return x * cos + partner * sin


def _mm_body(*refs, nk, epilogue, n_extra):
    x_ref, w_ref = refs[0], refs[1]
    extra = refs[2:2 + n_extra]
    o_ref = refs[2 + n_extra]
    part = jnp.dot(x_ref[...], w_ref[...], preferred_element_type=F32)

    def finish(acc):
        epilogue(acc, o_ref, *extra)

    if nk == 1:
        finish(part)
        return
    acc_ref = refs[3 + n_extra]
    k = pl.program_id(2)

    @pl.when(k == 0)
    def _():
        acc_ref[...] = part

    @pl.when(k > 0)
    def _():
        acc_ref[...] += part

    @pl.when(k == nk - 1)
    def _():
        finish(acc_ref[...])


def _epi_identity(acc, o_ref):
    o_ref[...] = acc.astype(o_ref.dtype)


def _epi_relu2(acc, o_ref):
    r = jnp.maximum(acc, 0.0)
    o_ref[...] = (r * r).astype(o_ref.dtype)


def _epi_rope(acc, o_ref, cos_ref, sin_ref, *, tn, lo, hi):
    col0 = pl.program_id(1) * tn
    in_rope = (col0 >= lo) & (col0 < hi)

    @pl.when(in_rope)
    def _():
        o_ref[...] = _rope(acc, cos_ref[...], sin_ref[...]).astype(o_ref.dtype)

    @pl.when(jnp.logical_not(in_rope))
    def _():
        o_ref[...] = acc.astype(o_ref.dtype)


def _matmul(x, w, out_dtype, *, epilogue=_epi_identity, rope=None, tm_target=1088, tn_target=512,
            tk_target=4096, name="matmul"):
    M, K = x.shape
    N = w.shape[1]
    if rope is None:
        tm = _pick_tile(M, tm_target, 16)
        tn = _pick_tile(N, tn_target, LANES)
    else:
        tm = _pick_tile(rope[2], tm_target, 16)
        tn = _pick_tile(math.gcd(N, rope[3], rope[4]), tn_target, LANES)
    tk = _pick_tile(K, tk_target, LANES)
    nk = K // tk
    in_specs = [pl.BlockSpec((tm, tk), lambda i, j, k: (i, k)),
                pl.BlockSpec((tk, tn), lambda i, j, k: (k, j))]
    operands = [x, w]
    n_extra = 0
    if rope is not None:
        cos, sin, rows_per_batch, lo, hi = rope
        assert rows_per_batch % tm == 0 and lo % tn == 0 and hi % tn == 0
        per = rows_per_batch // tm
        tab_spec = pl.BlockSpec((tm, LANES), lambda i, j, k: (i % per, 0))
        in_specs += [tab_spec, tab_spec]
        operands += [cos, sin]
        n_extra = 2
        epilogue = functools.partial(_epi_rope, tn=tn, lo=lo, hi=hi)
    scratch = [pltpu.VMEM((tm, tn), F32)] if nk > 1 else []
    return pl.pallas_call(
        functools.partial(_mm_body, nk=nk, epilogue=epilogue, n_extra=n_extra),
        grid=(M // tm, N // tn, nk),
        in_specs=in_specs,
        out_specs=pl.BlockSpec((tm, tn), lambda i, j, k: (i, j)),
        out_shape=jax.ShapeDtypeStruct((M, N), out_dtype),
        scratch_shapes=scratch,
        compiler_params=_params("parallel", "parallel", "arbitrary"),
        name=name,
    )(*operands)


def _mod_body(c_ref, w1_ref, w2_ref, b_ref, o_ref):
    c = c_ref[...]
    s = c * (1.0 / (1.0 + jnp.exp(-c)))
    h = jnp.dot(s.astype(BF16), w1_ref[...].astype(BF16), preferred_element_type=F32)
    o = jnp.dot(h.astype(BF16), w2_ref[...].astype(BF16), preferred_element_type=F32)
    o_ref[...] = o + b_ref[...]


def _modulation(cond, w1, w2, b):
    R, D = cond.shape
    rank = w1.shape[1]
    N = w2.shape[1]
    tn = _pick_tile(N, 2048, LANES)
    out = pl.pallas_call(
        _mod_body,
        grid=(N // tn,),
        in_specs=[pl.BlockSpec((R, D), lambda j: (0, 0)),
                  pl.BlockSpec((D, rank), lambda j: (0, 0)),
                  pl.BlockSpec((rank, tn), lambda j: (0, j)),
                  pl.BlockSpec((1, tn), lambda j: (0, j))],
        out_specs=pl.BlockSpec((R, tn), lambda j: (0, j)),
        out_shape=jax.ShapeDtypeStruct((R, N), F32),
        compiler_params=_params("arbitrary"),
        name="adaln_modulation",
    )(cond, w1, w2, b.reshape(1, N))
    return out.reshape(R, N_MOD, D)


def _modulate_body(x_ref, mod_ref, h_ref, *, scale_idx, shift_idx):
    x = x_ref[0]
    scale = mod_ref[0, scale_idx:scale_idx + 1, :]
    shift = mod_ref[0, shift_idx:shift_idx + 1, :]
    h_ref[0] = (x * (1.0 + scale) + shift).astype(h_ref.dtype)


def _modulate(x, mod, n_ctx_tiles, *, scale_idx, shift_idx):
    B, T, D = x.shape
    ctx_row = B
    return pl.pallas_call(
        functools.partial(_modulate_body, scale_idx=scale_idx, shift_idx=shift_idx),
        grid=(B, T // ROW_TILE),
        in_specs=[pl.BlockSpec((1, ROW_TILE, D), lambda b, t: (b, t, 0)),
                  pl.BlockSpec((1, N_MOD, D), lambda b, t: (jnp.where(t < n_ctx_tiles, ctx_row, b), 0, 0))],
        out_specs=pl.BlockSpec((1, ROW_TILE, D), lambda b, t: (b, t, 0)),
        out_shape=jax.ShapeDtypeStruct((B, T, D), BF16),
        compiler_params=_params("parallel", "parallel"),
        name="modulate",
    )(x, mod)


def _ln_mod_body(x_ref, y_ref, modg_ref, lng_ref, lnb_ref, *rest, alpha, gate_idx, ln_idx, scale_idx,
                 shift_idx, want_h):
    if want_h:
        modn_ref, xo_ref, ho_ref = rest
    else:
        (xo_ref,) = rest
    x = x_ref[0]
    y = y_ref[0]
    gate = modg_ref[0, gate_idx:gate_idx + 1, :]
    z = alpha * x + gate * y
    mu = jnp.mean(z, axis=-1, keepdims=True)
    zc = z - mu
    var = jnp.mean(zc * zc, axis=-1, keepdims=True)
    xn = (zc * lax.rsqrt(var + LN_EPS)) * lng_ref[ln_idx:ln_idx + 1, :] + lnb_ref[ln_idx:ln_idx + 1, :]
    xo_ref[0] = xn
    if want_h:
        scale = modn_ref[0, scale_idx:scale_idx + 1, :]
        shift = modn_ref[0, shift_idx:shift_idx + 1, :]
        ho_ref[0] = (xn * (1.0 + scale) + shift).astype(ho_ref.dtype)


def _ln_mod(x, y, mod_gate, ln_g, ln_b, mod_next, n_ctx_tiles, *, alpha, gate_idx, ln_idx, scale_idx=0,
            shift_idx=0, latent_only=False):
    B, T, D = x.shape
    ctx_row = B
    want_h = mod_next is not None
    off = n_ctx_tiles if latent_only else 0
    n_tiles = T // ROW_TILE - off
    row_in = pl.BlockSpec((1, ROW_TILE, D), lambda b, t: (b, t + off, 0))
    row_out = pl.BlockSpec((1, ROW_TILE, D), lambda b, t: (b, t, 0))
    mod_spec = pl.BlockSpec((1, N_MOD, D), lambda b, t: (jnp.where(t + off < n_ctx_tiles, ctx_row, b), 0, 0))
    ln_spec = pl.BlockSpec((2, D), lambda b, t: (0, 0))
    in_specs = [row_in, row_in, mod_spec, ln_spec, ln_spec]
    operands = [x, y, mod_gate, ln_g, ln_b]
    out_specs = [row_out]
    out_shape = [jax.ShapeDtypeStruct((B, n_tiles * ROW_TILE, D), F32)]
    if want_h:
        in_specs.append(mod_spec)
        operands.append(mod_next)
        out_specs.append(row_out)
        out_shape.append(jax.ShapeDtypeStruct((B, n_tiles * ROW_TILE, D), BF16))
    res = pl.pallas_call(
        functools.partial(_ln_mod_body, alpha=alpha, gate_idx=gate_idx, ln_idx=ln_idx, scale_idx=scale_idx,
                          shift_idx=shift_idx, want_h=want_h),
        grid=(B, n_tiles),
        in_specs=in_specs,
        out_specs=out_specs,
        out_shape=out_shape,
        compiler_params=_params("parallel", "parallel"),
        name="resid_ln_modulate",
    )(*operands)
    return (res[0], res[1]) if want_h else (res[0], None)


def _scores(q, k):
    return lax.dot_general(q, k, (((1,), (1,)), ((), ())), preferred_element_type=F32)


def _softmax_pv(s, v):
    m = jnp.max(s, axis=-1, keepdims=True)
    p = jnp.exp(s - m)
    l = jnp.sum(p, axis=-1, keepdims=True)
    o = jnp.dot(p.astype(BF16), v, preferred_element_type=F32)
    return o / l


def _da_body(lam_ref, q_ref, k_ref, v_ref, g_ref, o_ref, *, n_ctx_tiles, n_ctx, lam_init):
    qi = pl.program_id(2)
    lp = lam_ref[...]
    lam = (jnp.exp(jnp.sum(lp[0:1] * lp[1:2], axis=-1, keepdims=True))
           - jnp.exp(jnp.sum(lp[2:3] * lp[3:4], axis=-1, keepdims=True)) + lam_init)
    q = q_ref[0]
    lane = lax.broadcasted_iota(jnp.int32, q.shape, 1)
    zero = jnp.zeros_like(q)
    q0 = jnp.where(lane < DA_QK_DIM, q, zero)
    q1 = jnp.where(lane >= DA_QK_DIM, q, zero)
    scale = DA_QK_DIM ** -0.5

    def attend(n_keys):
        k = k_ref[0, :n_keys, :]
        v = v_ref[0, :n_keys, :]
        o0 = _softmax_pv(_scores(q0, k) * scale, v)
        o1 = _softmax_pv(_scores(q1, k) * scale, v)
        o = o0 - lam * o1
        ms = jnp.mean(o * o, axis=-1, keepdims=True)
        on = (o * lax.rsqrt(ms + DA_SUB_EPS)) * g_ref[...] * (1.0 - lam_init)
        o_ref[0] = on.astype(o_ref.dtype)

    @pl.when(qi < n_ctx_tiles)
    def _():
        attend(n_ctx)

    @pl.when(qi >= n_ctx_tiles)
    def _():
        attend(k_ref.shape[1])


def _diff_attention(qkv, lam_params, sub_g, n_ctx, lam_init):
    B, T, W = qkv.shape
    H = W // (3 * HEAD_V)
    nq = T // TQ
    return pl.pallas_call(
        functools.partial(_da_body, n_ctx_tiles=n_ctx // TQ, n_ctx=n_ctx, lam_init=lam_init),
        grid=(B, H, nq),
        in_specs=[pl.BlockSpec((8, LANES), lambda b, h, i: (0, 0)),
                  pl.BlockSpec((1, TQ, HEAD_V), lambda b, h, i: (b, i, h)),
                  pl.BlockSpec((1, T, HEAD_V), lambda b, h, i: (b, 0, H + h)),
                  pl.BlockSpec((1, T, HEAD_V), lambda b, h, i: (b, 0, 2 * H + h)),
                  pl.BlockSpec((1, HEAD_V), lambda b, h, i: (0, 0))],
        out_specs=pl.BlockSpec((1, TQ, HEAD_V), lambda b, h, i: (b, i, h)),
        out_shape=jax.ShapeDtypeStruct((B, T, H * HEAD_V), BF16),
        compiler_params=_params("parallel", "parallel", "arbitrary"),
        name="diff_attention",
    )(lam_params, qkv, qkv, qkv, sub_g)


def _mla_norm_body(d_ref, qg_ref, kvg_ref, cos_ref, sin_ref, cq_ref, ckv_ref, kr_ref, *, q_rank, kv_rank):
    d = d_ref[0]

    def rms(x, g):
        ms = jnp.mean(x * x, axis=-1, keepdims=True)
        return (x * lax.rsqrt(ms + RMS_EPS)) * g

    cq_ref[0] = rms(d[:, :q_rank], qg_ref[...]).astype(cq_ref.dtype)
    ckv_ref[0] = rms(d[:, q_rank:q_rank + kv_rank], kvg_ref[...]).astype(ckv_ref.dtype)
    kr = d[:, q_rank + kv_rank:q_rank + kv_rank + LANES]
    kr_ref[0] = _rope(kr, cos_ref[...], sin_ref[...]).astype(kr_ref.dtype)


def _mla_norm(down, q_g, kv_g, cos, sin):
    B, T, W = down.shape
    q_rank = q_g.shape[1]
    kv_rank = kv_g.shape[1]
    row = lambda w: pl.BlockSpec((1, ROW_TILE, w), lambda b, t: (b, t, 0))
    full = lambda w: pl.BlockSpec((1, w), lambda b, t: (0, 0))
    tab = pl.BlockSpec((ROW_TILE, LANES), lambda b, t: (t, 0))
    return pl.pallas_call(
        functools.partial(_mla_norm_body, q_rank=q_rank, kv_rank=kv_rank),
        grid=(B, T // ROW_TILE),
        in_specs=[row(W), full(q_rank), full(kv_rank), tab, tab],
        out_specs=[row(q_rank), row(kv_rank), row(LANES)],
        out_shape=[jax.ShapeDtypeStruct((B, T, q_rank), BF16),
                   jax.ShapeDtypeStruct((B, T, kv_rank), BF16),
                   jax.ShapeDtypeStruct((B, T, LANES), BF16)],
        compiler_params=_params("parallel", "parallel"),
        name="mla_norm_rope",
    )(down, q_g, kv_g, cos, sin)


def _mla_body(qn_ref, qr_ref, kn_ref, kr_ref, v_ref, o_ref, *, n_ctx_tiles, n_ctx):
    h = pl.program_id(1)
    qi = pl.program_id(2)
    qr = qr_ref[0]
    lane = lax.broadcasted_iota(jnp.int32, qr.shape, 1)
    mine = (lane // MLA_ROPE) == (h % 2)
    q = jnp.concatenate([qn_ref[0], jnp.where(mine, qr, jnp.zeros_like(qr))], axis=1)
    scale = (MLA_NOPE + MLA_ROPE) ** -0.5

    def attend(n_keys):
        k = jnp.concatenate([kn_ref[0, :n_keys, :], kr_ref[0, :n_keys, :]], axis=1)
        o = _softmax_pv(_scores(q, k) * scale, v_ref[0, :n_keys, :])
        o_ref[0] = o.astype(o_ref.dtype)

    @pl.when(qi < n_ctx_tiles)
    def _():
        attend(n_ctx)

    @pl.when(qi >= n_ctx_tiles)
    def _():
        attend(kn_ref.shape[1])


def _mla_attention(q, kv, kr, n_ctx):
    B, T, Wq = q.shape
    H = Wq // (MLA_NOPE + MLA_ROPE)
    nq = T // TQ
    return pl.pallas_call(
        functools.partial(_mla_body, n_ctx_tiles=n_ctx // TQ, n_ctx=n_ctx),
        grid=(B, H, nq),
        in_specs=[pl.BlockSpec((1, TQ, LANES), lambda b, h, i: (b, i, h)),
                  pl.BlockSpec((1, TQ, LANES), lambda b, h, i: (b, i, H + h // 2)),
                  pl.BlockSpec((1, T, LANES), lambda b, h, i: (b, 0, h)),
                  pl.BlockSpec((1, T, LANES), lambda b, h, i: (b, 0, 0)),
                  pl.BlockSpec((1, T, LANES), lambda b, h, i: (b, 0, H + h))],
        out_specs=pl.BlockSpec((1, TQ, HEAD_V), lambda b, h, i: (b, i, h)),
        out_shape=jax.ShapeDtypeStruct((B, T, H * HEAD_V), BF16),
        compiler_params=_params("parallel", "parallel", "arbitrary"),
        name="mla_attention",
    )(q, q, kv, kr, kv)


def _na_key_row_start(i, n_rows):
    return jnp.clip(NA_TILE_ROWS * i - NA_WIN_ROWS // 2, 0, n_rows - NA_KEY_ROWS)


def _na_body(q_ref, k_ref, v_ref, bias_ref, o_ref, *, n_ctx_tiles, n_ctx, n_rows):
    qi = pl.program_id(2)
    q = q_ref[0]
    scale = HEAD_V ** -0.5

    @pl.when(qi < n_ctx_tiles)
    def _():
        o = _softmax_pv(_scores(q, k_ref[0, :n_ctx, :]) * scale, v_ref[0, :n_ctx, :])
        o_ref[0] = o.astype(o_ref.dtype)

    @pl.when(qi >= n_ctx_tiles)
    def _():
        n_lat = NA_KEY_ROWS * GRID_W
        start = pl.multiple_of(n_ctx + _na_key_row_start(qi - n_ctx_tiles, n_rows) * GRID_W, GRID_W)
        s_ctx = _scores(q, k_ref[0, :n_ctx, :]) * scale
        s_lat = _scores(q, k_ref[0, pl.ds(start, n_lat), :]) * scale + bias_ref[0, 0]
        m = jnp.maximum(jnp.max(s_ctx, axis=-1, keepdims=True), jnp.max(s_lat, axis=-1, keepdims=True))
        p_ctx = jnp.exp(s_ctx - m)
        p_lat = jnp.exp(s_lat - m)
        l = jnp.sum(p_ctx, axis=-1, keepdims=True) + jnp.sum(p_lat, axis=-1, keepdims=True)
        o = (jnp.dot(p_ctx.astype(BF16), v_ref[0, :n_ctx, :], preferred_element_type=F32)
             + jnp.dot(p_lat.astype(BF16), v_ref[0, pl.ds(start, n_lat), :], preferred_element_type=F32))
        o_ref[0] = (o / l).astype(o_ref.dtype)


def _na_bias_table(rpb, n_rows):
    n_tiles = n_rows // NA_TILE_ROWS
    tiles = jnp.array([0, 1, n_tiles - 1], jnp.int32)
    ks = _na_key_row_start(tiles, n_rows)
    qrow = NA_TILE_ROWS * tiles[:, None] + jnp.arange(NA_TILE_ROWS)[None]
    krow = ks[:, None] + jnp.arange(NA_KEY_ROWS)[None]
    r0 = jnp.clip(qrow - NA_WIN_ROWS // 2, 0, n_rows - NA_WIN_ROWS)
    row_in = (krow[:, None, :] >= r0[:, :, None]) & (krow[:, None, :] < r0[:, :, None] + NA_WIN_ROWS)
    row_idx = jnp.clip(krow[:, None, :] - qrow[:, :, None] + NA_WIN_ROWS - 1, 0, 2 * NA_WIN_ROWS - 2)
    col = jnp.arange(GRID_W)
    col_start = jnp.clip(col - NA_WIN_COLS // 2, 0, GRID_W - NA_WIN_COLS)
    col_in = (col[None, :] >= col_start[:, None]) & (col[None, :] < col_start[:, None] + NA_WIN_COLS)
    col_idx = jnp.clip(col[None, :] - col[:, None] + NA_WIN_COLS - 1, 0, 2 * NA_WIN_COLS - 2)
    b = rpb[:, row_idx[:, :, :, None, None], col_idx[None, None, None, :, :]]
    ok = row_in[:, :, :, None, None] & col_in[None, None, None, :, :]
    b = jnp.where(ok[None], b, MASK_BIAS).astype(F32)
    b = b.transpose(0, 1, 2, 4, 3, 5)
    return b.reshape(rpb.shape[0], 3, TQ, NA_KEY_ROWS * GRID_W)


def _na_attention(qkv, bias, n_ctx):
    B, T, W = qkv.shape
    H = W // (3 * HEAD_V)
    nq = T // TQ
    n_ctx_tiles = n_ctx // TQ
    n_rows = (T - n_ctx) // GRID_W
    n_lat_tiles = nq - n_ctx_tiles

    def bias_index(b, h, i):
        t = i - n_ctx_tiles
        return (h, jnp.where(t <= 0, 0, jnp.where(t == n_lat_tiles - 1, 2, 1)), 0, 0)

    return pl.pallas_call(
        functools.partial(_na_body, n_ctx_tiles=n_ctx_tiles, n_ctx=n_ctx, n_rows=n_rows),
        grid=(B, H, nq),
        in_specs=[pl.BlockSpec((1, TQ, HEAD_V), lambda b, h, i: (b, i, h)),
                  pl.BlockSpec((1, T, HEAD_V), lambda b, h, i: (b, 0, H + h)),
                  pl.BlockSpec((1, T, HEAD_V), lambda b, h, i: (b, 0, 2 * H + h)),
                  pl.BlockSpec((1, 1, TQ, NA_KEY_ROWS * GRID_W), bias_index)],
        out_specs=pl.BlockSpec((1, TQ, HEAD_V), lambda b, h, i: (b, i, h)),
        out_shape=jax.ShapeDtypeStruct((B, T, H * HEAD_V), BF16),
        compiler_params=_params("parallel", "parallel", "arbitrary"),
        name="neighbourhood_attention",
    )(qkv, qkv, qkv, bias)


def _rope_tables(n_ctx, seq):
    pos = jnp.arange(seq)
    quarter = DA_QK_DIM // 4
    inv = ROPE_BASE ** (-jnp.arange(quarter, dtype=F32) / quarter)
    ang_r = (pos // GRID_W).astype(F32)[:, None] * inv
    ang_c = (pos % GRID_W).astype(F32)[:, None] * inv
    cos = jnp.concatenate([jnp.cos(ang_r)] * 2 + [jnp.cos(ang_c)] * 2, axis=1)
    sin = jnp.concatenate([-jnp.sin(ang_r), jnp.sin(ang_r), -jnp.sin(ang_c), jnp.sin(ang_c)], axis=1)
    cos = jnp.concatenate([jnp.ones((n_ctx, DA_QK_DIM), F32), cos], axis=0)
    sin = jnp.concatenate([jnp.zeros((n_ctx, DA_QK_DIM), F32), sin], axis=0)
    return jnp.concatenate([cos, cos], axis=1), jnp.concatenate([sin, sin], axis=1)


def kernel(x, c, ctx, c_ctx, ada_w1, ada_w2, ada_b, ln_g, ln_b, mlp_w1, mlp_w2, da_w_qkv, da_w_o, da_lq1,
           da_lk1, da_lq2, da_lk2, da_sub_g, mla_w_down, mla_q_g, mla_w_uq, mla_kv_g, mla_w_ukv, mla_w_o,
           na_w_qkv, na_w_o, na_rpb):
    B, S, D = x.shape
    C = ctx.shape[1]
    T = C + S
    M = B * T
    depth = ada_w1.shape[0]
    alpha = (2.0 * depth) ** 0.25
    n_ctx_tiles = C // ROW_TILE
    assert C % TQ == 0 and S % TQ == 0 and ROW_TILE == TQ and B + 1 <= 8

    cos, sin = _rope_tables(C, S)
    xs = jnp.concatenate([ctx, x], axis=1)
    cond = jnp.concatenate([c, c_ctx[None], jnp.zeros((8 - B - 1, D), F32)], axis=0)
    mods = [_modulation(cond, ada_w1[i], ada_w2[i], ada_b[i]) for i in range(depth)]

    h = _modulate(xs, mods[0], n_ctx_tiles, scale_idx=1, shift_idx=0)
    out = None
    for i in range(depth):
        kind, slot = i % N_MIXERS, i // N_MIXERS
        h2 = h.reshape(M, D)
        if kind == 0:
            lam_init = 0.8 - 0.6 * math.exp(-0.3 * i)
            n_heads = da_w_o.shape[1] // HEAD_V
            qk_cols = 2 * n_heads * HEAD_V
            qkv = _matmul(h2, da_w_qkv[slot].astype(BF16), BF16, rope=(cos, sin, T, 0, qk_cols),
                          name="da_qkv")
            lam_params = jnp.zeros((8, LANES), F32).at[:4, :DA_QK_DIM].set(
                jnp.stack([da_lq1[slot], da_lk1[slot], da_lq2[slot], da_lk2[slot]]))
            o = _diff_attention(qkv.reshape(B, T, -1), lam_params, da_sub_g[slot][None], C, lam_init)
            w_o = da_w_o[slot]
        elif kind == 1:
            w_down = mla_w_down[slot]
            q_rank = mla_q_g.shape[1]
            kv_rank = mla_kv_g.shape[1]
            n_heads = mla_w_o.shape[1] // HEAD_V
            k_rope_w = w_down[:, q_rank + kv_rank:]
            w_down = jnp.concatenate([w_down, k_rope_w, jnp.zeros((D, LANES), F32)], axis=1)
            down = _matmul(h2, w_down.astype(BF16), F32, name="mla_down")
            cq, ckv, kr = _mla_norm(down.reshape(B, T, -1), mla_q_g[slot][None], mla_kv_g[slot][None], cos, sin)
            w_uq = mla_w_uq[slot].reshape(q_rank, n_heads, MLA_NOPE + MLA_ROPE)
            w_uq = jnp.concatenate([w_uq[:, :, :MLA_NOPE].reshape(q_rank, -1),
                                    w_uq[:, :, MLA_NOPE:].reshape(q_rank, -1)], axis=1)
            nope_cols = n_heads * MLA_NOPE
            q = _matmul(cq.reshape(M, q_rank), w_uq.astype(BF16), BF16,
                        rope=(cos, sin, T, nope_cols, nope_cols + n_heads * MLA_ROPE), name="mla_uq")
            w_ukv = mla_w_ukv[slot].reshape(kv_rank, n_heads, MLA_NOPE + HEAD_V)
            w_ukv = jnp.concatenate([w_ukv[:, :, :MLA_NOPE].reshape(kv_rank, -1),
                                     w_ukv[:, :, MLA_NOPE:].reshape(kv_rank, -1)], axis=1)
            kv = _matmul(ckv.reshape(M, kv_rank), w_ukv.astype(BF16), BF16, name="mla_ukv")
            o = _mla_attention(q.reshape(B, T, -1), kv.reshape(B, T, -1), kr, C)
            w_o = mla_w_o[slot]
        else:
            qkv = _matmul(h2, na_w_qkv[slot].astype(BF16), BF16, name="na_qkv")
            bias = _na_bias_table(na_rpb[slot], S // GRID_W)
            o = _na_attention(qkv.reshape(B, T, -1), bias, C)
            w_o = na_w_o[slot]
        y = _matmul(o.reshape(M, -1), w_o.astype(BF16), F32, name="attn_out").reshape(B, T, D)
        xs, h = _ln_mod(xs, y, mods[i], ln_g[i], ln_b[i], mods[i], n_ctx_tiles, alpha=alpha, gate_idx=2,
                        ln_idx=0, scale_idx=4, shift_idx=3)
        u = _matmul(h.reshape(M, D), mlp_w1[i].astype(BF16), BF16, epilogue=_epi_relu2, name="mlp_up")
        y = _matmul(u, mlp_w2[i].astype(BF16), F32, tk_target=2048, name="mlp_down").reshape(B, T, D)
        if i + 1 < depth:
            xs, h = _ln_mod(xs, y, mods[i], ln_g[i], ln_b[i], mods[i + 1], n_ctx_tiles, alpha=alpha,
                            gate_idx=5, ln_idx=1, scale_idx=1, shift_idx=0)
        else:
            out, _ = _ln_mod(xs, y, mods[i], ln_g[i], ln_b[i], None, n_ctx_tiles, alpha=alpha, gate_idx=5,
                             ln_idx=1, latent_only=True)
    return out
```

```python
import functools
import math

import numpy as np
import jax
import jax.numpy as jnp
from jax import lax
from jax.experimental import pallas as pl
from jax.experimental.pallas import tpu as pltpu

F32 = jnp.float32
BF16 = jnp.bfloat16

GRID_W = 64
N_MOD = 6
ROPE_BASE = 10000.0
LN_EPS = 1e-5
RMS_EPS = 1e-6
DA_SUB_EPS = 1e-5
DA_QK_DIM = 64
HEAD_V = 128
MLA_NOPE = 128
MLA_ROPE = 64
NA_WIN_ROWS = 8
NA_WIN_COLS = 16
N_MIXERS = 3
MASK_BIAS = -1e30
LOG2E = 1.4426950408889634

LANES = 128
MXU_DIM = 256
VMEM_LIMIT_BYTES = 48 * 1024 * 1024

TQ = 256
KEY_CHUNK = 2 * MXU_DIM
FLASH_LOOKAHEAD = 6
NA_TILE_ROWS = TQ // GRID_W
NA_KEY_ROWS = 12
ROW_TILE = 256
MM_TM = 1088
MM_TN = 512
MM_TK_FULL = 4096
MM_TK = 2048
MM_ROW_SPLITS = 4


def _pick_tile(dim, target, align):
    best = None
    for t in range(align, min(dim, target) + 1, align):
        if dim % t == 0:
            best = t
    return best if best is not None else dim


def _params(*sem):
    return pltpu.CompilerParams(dimension_semantics=sem, vmem_limit_bytes=VMEM_LIMIT_BYTES)


def _rope(x, cos, sin):
    n = x.shape[1]
    reps = n // LANES
    if reps > 1:
        cos = jnp.concatenate([cos] * reps, axis=1)
        sin = jnp.concatenate([sin] * reps, axis=1)
    lane = lax.broadcasted_iota(jnp.int32, x.shape, 1)
    nxt = pltpu.roll(x, n - 16, 1)
    prv = pltpu.roll(x, 16, 1)
    partner = jnp.where((lane & 16) == 0, nxt, prv)
    return x * cos + partner * sin


def _mm_body(*refs, nk, epilogue, n_extra, splits):
    x_ref, w_ref = refs[0], refs[1]
    extra = refs[2:2 + n_extra]
    o_ref = refs[2 + n_extra]
    wb_ref = refs[3 + n_extra]
    k = pl.program_id(1)
    i = pl.program_id(2)

    @pl.when(i == 0)
    def _():
        wb_ref[...] = w_ref[...].astype(BF16)

    if nk > 1:
        acc_ref = refs[4 + n_extra]

        @pl.when((pl.program_id(0) == 0) & (k == 0))
        def _():
            acc_ref[i] = jnp.zeros(acc_ref.shape[1:], F32)

    tm = x_ref.shape[0]
    sub = tm // splits
    for r in range(splits):
        rows = slice(r * sub, (r + 1) * sub)
        acc = jnp.dot(x_ref[rows, :], wb_ref[...], preferred_element_type=F32)
        if nk > 1:
            acc = jnp.where(k > 0, acc_ref[i, rows, :], 0.0) + acc
            acc_ref[i, rows, :] = acc
        epilogue(acc, rows, o_ref, *extra)


def _epi_identity(acc, rows, o_ref):
    o_ref[rows, :] = acc.astype(o_ref.dtype)


def _epi_relu2(acc, rows, o_ref):
    r = jnp.maximum(acc, 0.0)
    o_ref[rows, :] = (r * r).astype(o_ref.dtype)


def _epi_rope(acc, rows, o_ref, cos_ref, sin_ref):
    o_ref[rows, :] = _rope(acc, cos_ref[rows, :], sin_ref[rows, :]).astype(o_ref.dtype)


def _matmul(x, w, layer, out_dtype, *, epilogue=_epi_identity, rope=None, name="matmul"):
    M, K = x.shape
    N = w.shape[2]
    if rope is None:
        tm = _pick_tile(M, MM_TM, 16)
        tn = _pick_tile(N, MM_TN, LANES)
    else:
        tm = _pick_tile(rope[2], MM_TM, 16)
        tn = _pick_tile(math.gcd(N, rope[3], rope[4]), MM_TN, LANES)
    tk = K if K <= MM_TK_FULL else _pick_tile(K, MM_TK, LANES)
    nk = K // tk
    nm = M // tm
    last = nk - 1
    in_specs = [pl.BlockSpec((tm, tk), lambda j, k, i: (i, k)),
                pl.BlockSpec((None, tk, tn), lambda j, k, i: (layer, k, j))]
    operands = [x, w]
    n_extra = 0
    if rope is not None:
        cos, sin, rows_per_batch, lo, hi = rope
        per = rows_per_batch // tm
        tab_spec = pl.BlockSpec(
            (None, tm, LANES),
            lambda j, k, i: (jnp.where((j * tn >= lo) & (j * tn < hi), 0, 1), i % per, 0))
        in_specs += [tab_spec, tab_spec]
        operands += [cos, sin]
        n_extra = 2
        epilogue = _epi_rope
    scratch = [pltpu.VMEM((tk, tn), BF16)]
    if nk > 1:
        scratch.append(pltpu.VMEM((nm, tm, tn), F32))
    out_spec = pl.BlockSpec((tm, tn), lambda j, k, i: (jnp.where(k == last, i, 0), j))
    return pl.pallas_call(
        functools.partial(_mm_body, nk=nk, epilogue=epilogue, n_extra=n_extra,
                          splits=MM_ROW_SPLITS if tm % (16 * MM_ROW_SPLITS) == 0 else 1),
        grid=(N // tn, nk, nm),
        in_specs=in_specs,
        out_specs=out_spec,
        out_shape=jax.ShapeDtypeStruct((M, N), out_dtype),
        scratch_shapes=scratch,
        compiler_params=_params("parallel", "arbitrary", "arbitrary"),
        name=name,
    )(*operands)


def _mod_body(c_ref, w1_ref, w2_ref, b_ref, o_ref):
    c = c_ref[...]
    s = c * (1.0 / (1.0 + jnp.exp(-c)))
    h = jnp.dot(s.astype(BF16), w1_ref[...].astype(BF16), preferred_element_type=F32)
    o = jnp.dot(h.astype(BF16), w2_ref[...].astype(BF16), preferred_element_type=F32)
    o_ref[...] = o + b_ref[...]


def _modulation(cond, w1, w2, b, layer):
    R, D = cond.shape
    rank = w1.shape[2]
    N = w2.shape[2]
    tn = _pick_tile(N, 2048, LANES)
    out = pl.pallas_call(
        _mod_body,
        grid=(N // tn,),
        in_specs=[pl.BlockSpec((R, D), lambda j: (0, 0)),
                  pl.BlockSpec((None, D, rank), lambda j: (layer, 0, 0)),
                  pl.BlockSpec((None, rank, tn), lambda j: (layer, 0, j)),
                  pl.BlockSpec((None, 1, tn), lambda j: (layer, 0, j))],
        out_specs=pl.BlockSpec((R, tn), lambda j: (0, j)),
        out_shape=jax.ShapeDtypeStruct((R, N), F32),
        compiler_params=_params("arbitrary"),
        name="adaln_modulation",
    )(cond, w1, w2, b.reshape(b.shape[0], 1, N))
    return out.reshape(R, N_MOD, D)


def _modulate_body(x_ref, mod_ref, h_ref, *, scale_idx, shift_idx):
    x = x_ref[0]
    scale = mod_ref[0, scale_idx:scale_idx + 1, :]
    shift = mod_ref[0, shift_idx:shift_idx + 1, :]
    h_ref[0] = (x * (1.0 + scale) + shift).astype(h_ref.dtype)


def _modulate(x, mod, n_ctx_tiles, *, scale_idx, shift_idx):
    B, T, D = x.shape
    ctx_row = B
    return pl.pallas_call(
        functools.partial(_modulate_body, scale_idx=scale_idx, shift_idx=shift_idx),
        grid=(B, T // ROW_TILE),
        in_specs=[pl.BlockSpec((1, ROW_TILE, D), lambda b, t: (b, t, 0)),
                  pl.BlockSpec((1, N_MOD, D), lambda b, t: (jnp.where(t < n_ctx_tiles, ctx_row, b), 0, 0))],
        out_specs=pl.BlockSpec((1, ROW_TILE, D), lambda b, t: (b, t, 0)),
        out_shape=jax.ShapeDtypeStruct((B, T, D), BF16),
        compiler_params=_params("parallel", "parallel"),
        name="modulate",
    )(x, mod)


def _ln_mod_body(x_ref, y_ref, modg_ref, lng_ref, lnb_ref, *rest, alpha, gate_idx, ln_idx, scale_idx,
                 shift_idx, want_h):
    if want_h:
        modn_ref, xo_ref, ho_ref = rest
    else:
        (xo_ref,) = rest
    x = x_ref[0]
    y = y_ref[0]
    gate = modg_ref[0, gate_idx:gate_idx + 1, :]
    z = alpha * x + gate * y
    mu = jnp.mean(z, axis=-1, keepdims=True)
    zc = z - mu
    var = jnp.mean(zc * zc, axis=-1, keepdims=True)
    xn = (zc * lax.rsqrt(var + LN_EPS)) * lng_ref[ln_idx:ln_idx + 1, :] + lnb_ref[ln_idx:ln_idx + 1, :]
    xo_ref[0] = xn
    if want_h:
        scale = modn_ref[0, scale_idx:scale_idx + 1, :]
        shift = modn_ref[0, shift_idx:shift_idx + 1, :]
        ho_ref[0] = (xn * (1.0 + scale) + shift).astype(ho_ref.dtype)


def _ln_mod(x, y, mod_gate, ln_g, ln_b, mod_next, n_ctx_tiles, *, layer, alpha, gate_idx, ln_idx, scale_idx=0,
            shift_idx=0, latent_only=False):
    B, T, D = x.shape
    ctx_row = B
    want_h = mod_next is not None
    off = n_ctx_tiles if latent_only else 0
    n_tiles = T // ROW_TILE - off
    row_in = pl.BlockSpec((1, ROW_TILE, D), lambda b, t: (b, t + off, 0))
    row_out = pl.BlockSpec((1, ROW_TILE, D), lambda b, t: (b, t, 0))
    mod_spec = pl.BlockSpec((1, N_MOD, D), lambda b, t: (jnp.where(t + off < n_ctx_tiles, ctx_row, b), 0, 0))
    ln_spec = pl.BlockSpec((None, 2, D), lambda b, t: (layer, 0, 0))
    in_specs = [row_in, row_in, mod_spec, ln_spec, ln_spec]
    operands = [x, y, mod_gate, ln_g, ln_b]
    out_specs = [row_out]
    out_shape = [jax.ShapeDtypeStruct((B, n_tiles * ROW_TILE, D), F32)]
    if want_h:
        in_specs.append(mod_spec)
        operands.append(mod_next)
        out_specs.append(row_out)
        out_shape.append(jax.ShapeDtypeStruct((B, n_tiles * ROW_TILE, D), BF16))
    res = pl.pallas_call(
        functools.partial(_ln_mod_body, alpha=alpha, gate_idx=gate_idx, ln_idx=ln_idx, scale_idx=scale_idx,
                          shift_idx=shift_idx, want_h=want_h),
        grid=(B, n_tiles),
        in_specs=in_specs,
        out_specs=out_specs,
        out_shape=out_shape,
        compiler_params=_params("parallel", "parallel"),
        name="resid_ln_modulate",
    )(*operands)
    return (res[0], res[1]) if want_h else (res[0], None)


def _flash(items, c):
    def scores(it):
        return lax.dot_general(it["k"](), it["q"], (((1,), (1,)), ((), ())), preferred_element_type=F32)

    def pv(it, p):
        return lax.dot_general(it["v"](), p.astype(BF16), (((0,), (0,)), ((), ())), preferred_element_type=F32)

    queue = [scores(it) for it in items[:FLASH_LOOKAHEAD]]
    state = None
    for idx, it in enumerate(items):
        s = queue.pop(0)
        if idx + FLASH_LOOKAHEAD < len(items):
            queue.append(scores(items[idx + FLASH_LOOKAHEAD]))
        cm = jnp.max(s, axis=0, keepdims=True)
        if it["first"]:
            m = cm
            p = jnp.exp2((s - m) * c)
            l = jnp.sum(p, axis=0, keepdims=True)
            acc = pv(it, p)
        else:
            m_old, l_old, acc_old = state
            m = jnp.maximum(m_old, cm)
            alpha = jnp.exp2((m_old - m) * c)
            p = jnp.exp2((s - m) * c)
            l = alpha * l_old + jnp.sum(p, axis=0, keepdims=True)
            acc = alpha * acc_old + pv(it, p)
        state = (m, l, acc)
        if it["last"]:
            it["emit"]((acc / l).T)


def _key_chunks(n_ctx, n_keys):
    return [(0, n_ctx)] + [(lo, KEY_CHUNK) for lo in range(n_ctx, n_keys, KEY_CHUNK)]


def _problem_items(q, k_get, v_get, chunks, emit):
    return [dict(q=q, k=functools.partial(k_get, lo, n), v=functools.partial(v_get, lo, n),
                 first=ci == 0, last=ci == len(chunks) - 1, emit=emit)
            for ci, (lo, n) in enumerate(chunks)]


def _for_ctx_and_latent_tiles(qi, n_ctx_tiles, n_ctx, n_keys, run):
    @pl.when(qi < n_ctx_tiles)
    def _():
        run(_key_chunks(n_ctx, n_ctx))

    @pl.when(qi >= n_ctx_tiles)
    def _():
        run(_key_chunks(n_ctx, n_keys))


def _da_body(lam_ref, q_ref, k_ref, v_ref, g_ref, o_ref, *, n_ctx_tiles, n_ctx, lam_init):
    lp = lam_ref[...]
    lam = (jnp.exp(jnp.sum(lp[0:1] * lp[1:2], axis=-1, keepdims=True))
           - jnp.exp(jnp.sum(lp[2:3] * lp[3:4], axis=-1, keepdims=True)) + lam_init)
    q = q_ref[0]
    lane = lax.broadcasted_iota(jnp.int32, q.shape, 1)
    zero = jnp.zeros_like(q)
    comps = [jnp.where(lane < DA_QK_DIM, q, zero), jnp.where(lane >= DA_QK_DIM, q, zero)]
    c = (DA_QK_DIM ** -0.5) * LOG2E

    def run(chunks):
        outs = []

        def finish(o1):
            o = outs[0] - lam * o1
            ms = jnp.mean(o * o, axis=-1, keepdims=True)
            on = (o * lax.rsqrt(ms + DA_SUB_EPS)) * g_ref[...] * (1.0 - lam_init)
            o_ref[0] = on.astype(o_ref.dtype)

        k_get = lambda lo, n: k_ref[0, lo:lo + n, :]
        v_get = lambda lo, n: v_ref[0, lo:lo + n, :]
        items = (_problem_items(comps[0], k_get, v_get, chunks, outs.append)
                 + _problem_items(comps[1], k_get, v_get, chunks, finish))
        _flash(items, c)

    _for_ctx_and_latent_tiles(pl.program_id(2), n_ctx_tiles, n_ctx, k_ref.shape[1], run)


def _diff_attention(qkv, lam_params, sub_g, n_ctx, lam_init):
    B, T, W = qkv.shape
    H = W // (3 * HEAD_V)
    nq = T // TQ
    return pl.pallas_call(
        functools.partial(_da_body, n_ctx_tiles=n_ctx // TQ, n_ctx=n_ctx, lam_init=lam_init),
        grid=(B, H, nq),
        in_specs=[pl.BlockSpec((8, LANES), lambda b, h, i: (0, 0)),
                  pl.BlockSpec((1, TQ, HEAD_V), lambda b, h, i: (b, i, h)),
                  pl.BlockSpec((1, T, HEAD_V), lambda b, h, i: (b, 0, H + h)),
                  pl.BlockSpec((1, T, HEAD_V), lambda b, h, i: (b, 0, 2 * H + h)),
                  pl.BlockSpec((1, HEAD_V), lambda b, h, i: (0, 0))],
        out_specs=pl.BlockSpec((1, TQ, HEAD_V), lambda b, h, i: (b, i, h)),
        out_shape=jax.ShapeDtypeStruct((B, T, H * HEAD_V), BF16),
        compiler_params=_params("parallel", "parallel", "arbitrary"),
        name="diff_attention",
    )(lam_params, qkv, qkv, qkv, sub_g)


def _mla_norm_body(d_ref, qg_ref, kvg_ref, cos_ref, sin_ref, cq_ref, ckv_ref, kr_ref, *, q_rank, kv_rank):
    d = d_ref[0]

    def rms(x, g):
        ms = jnp.mean(x * x, axis=-1, keepdims=True)
        return (x * lax.rsqrt(ms + RMS_EPS)) * g

    cq_ref[0] = rms(d[:, :q_rank], qg_ref[...]).astype(cq_ref.dtype)
    ckv_ref[0] = rms(d[:, q_rank:q_rank + kv_rank], kvg_ref[...]).astype(ckv_ref.dtype)
    kr = d[:, q_rank + kv_rank:q_rank + kv_rank + LANES]
    kr_ref[0] = _rope(kr, cos_ref[...], sin_ref[...]).astype(kr_ref.dtype)


def _mla_norm(down, q_g, kv_g, cos, sin):
    B, T, W = down.shape
    q_rank = q_g.shape[1]
    kv_rank = kv_g.shape[1]
    row = lambda w: pl.BlockSpec((1, ROW_TILE, w), lambda b, t: (b, t, 0))
    full = lambda w: pl.BlockSpec((1, w), lambda b, t: (0, 0))
    tab = pl.BlockSpec((None, ROW_TILE, LANES), lambda b, t: (0, t, 0))
    return pl.pallas_call(
        functools.partial(_mla_norm_body, q_rank=q_rank, kv_rank=kv_rank),
        grid=(B, T // ROW_TILE),
        in_specs=[row(W), full(q_rank), full(kv_rank), tab, tab],
        out_specs=[row(q_rank), row(kv_rank), row(LANES)],
        out_shape=[jax.ShapeDtypeStruct((B, T, q_rank), BF16),
                   jax.ShapeDtypeStruct((B, T, kv_rank), BF16),
                   jax.ShapeDtypeStruct((B, T, LANES), BF16)],
        compiler_params=_params("parallel", "parallel"),
        name="mla_norm_rope",
    )(down, q_g, kv_g, cos, sin)


MLA_HEADS_PER_STEP = 2


def _mla_body(qn_ref, qr_ref, kn_ref, kr_ref, v_ref, o_ref, *, n_ctx_tiles, n_ctx):
    qr = qr_ref[0]
    lane = lax.broadcasted_iota(jnp.int32, qr.shape, 1)
    c = ((MLA_NOPE + MLA_ROPE) ** -0.5) * LOG2E

    def run(chunks):
        items = []
        for j in range(MLA_HEADS_PER_STEP):
            cols = slice(j * HEAD_V, (j + 1) * HEAD_V)
            mine = (lane // MLA_ROPE) == j
            q = jnp.concatenate([qn_ref[0, :, cols], jnp.where(mine, qr, jnp.zeros_like(qr))], axis=1)

            def k_get(lo, n, cols=cols):
                return jnp.concatenate([kn_ref[0, lo:lo + n, cols], kr_ref[0, lo:lo + n, :]], axis=1)

            def v_get(lo, n, cols=cols):
                return v_ref[0, lo:lo + n, cols]

            def emit(o, cols=cols):
                o_ref[0, :, cols] = o.astype(o_ref.dtype)

            items += _problem_items(q, k_get, v_get, chunks, emit)
        _flash(items, c)

    _for_ctx_and_latent_tiles(pl.program_id(2), n_ctx_tiles, n_ctx, kn_ref.shape[1], run)


def _mla_attention(q, kv, kr, n_ctx):
    B, T, Wq = q.shape
    H = Wq // (MLA_NOPE + MLA_ROPE)
    nq = T // TQ
    G = MLA_HEADS_PER_STEP
    W = G * HEAD_V
    return pl.pallas_call(
        functools.partial(_mla_body, n_ctx_tiles=n_ctx // TQ, n_ctx=n_ctx),
        grid=(B, H // G, nq),
        in_specs=[pl.BlockSpec((1, TQ, W), lambda b, g, i: (b, i, g)),
                  pl.BlockSpec((1, TQ, LANES), lambda b, g, i: (b, i, H + g)),
                  pl.BlockSpec((1, T, W), lambda b, g, i: (b, 0, g)),
                  pl.BlockSpec((1, T, LANES), lambda b, g, i: (b, 0, 0)),
                  pl.BlockSpec((1, T, W), lambda b, g, i: (b, 0, H // G + g))],
        out_specs=pl.BlockSpec((1, TQ, W), lambda b, g, i: (b, i, g)),
        out_shape=jax.ShapeDtypeStruct((B, T, H * HEAD_V), BF16),
        compiler_params=_params("parallel", "parallel", "arbitrary"),
        name="mla_attention",
    )(q, q, kv, kr, kv)


def _scores(q, k):
    return lax.dot_general(q, k, (((1,), (1,)), ((), ())), preferred_element_type=F32)


def _na_key_row_start(i, n_rows):
    return jnp.clip(NA_TILE_ROWS * i - NA_WIN_ROWS // 2, 0, n_rows - NA_KEY_ROWS)


def _na_body(q_ref, k_ref, v_ref, bias_ref, o_ref, *, n_ctx_tiles, n_ctx, n_rows):
    qi = pl.program_id(2)
    q = q_ref[0]
    scale = HEAD_V ** -0.5

    @pl.when(qi < n_ctx_tiles)
    def _():
        s = _scores(q, k_ref[0, :n_ctx, :]) * scale
        p = jnp.exp(s - jnp.max(s, axis=-1, keepdims=True))
        o = jnp.dot(p.astype(BF16), v_ref[0, :n_ctx, :], preferred_element_type=F32)
        o_ref[0] = (o / jnp.sum(p, axis=-1, keepdims=True)).astype(o_ref.dtype)

    @pl.when(qi >= n_ctx_tiles)
    def _():
        n_lat = NA_KEY_ROWS * GRID_W
        start = pl.multiple_of(n_ctx + _na_key_row_start(qi - n_ctx_tiles, n_rows) * GRID_W, GRID_W)
        s_ctx = _scores(q, k_ref[0, :n_ctx, :]) * scale
        s_lat = _scores(q, k_ref[0, pl.ds(start, n_lat), :]) * scale + bias_ref[0, 0]
        m = jnp.maximum(jnp.max(s_ctx, axis=-1, keepdims=True), jnp.max(s_lat, axis=-1, keepdims=True))
        p_ctx = jnp.exp(s_ctx - m)
        p_lat = jnp.exp(s_lat - m)
        l = jnp.sum(p_ctx, axis=-1, keepdims=True) + jnp.sum(p_lat, axis=-1, keepdims=True)
        o = (jnp.dot(p_ctx.astype(BF16), v_ref[0, :n_ctx, :], preferred_element_type=F32)
             + jnp.dot(p_lat.astype(BF16), v_ref[0, pl.ds(start, n_lat), :], preferred_element_type=F32))
        o_ref[0] = (o / l).astype(o_ref.dtype)


def _na_bias_table(rpb, n_rows):
    H = rpb.shape[0]
    n_dr = 2 * NA_WIN_ROWS - 1
    W = GRID_W
    off_idx = np.clip(np.arange(2 * W - 1) - (W - 1) + NA_WIN_COLS - 1, 0, 2 * NA_WIN_COLS - 2)
    ext = rpb[:, :, off_idx]
    ext = jnp.broadcast_to(ext[:, :, None, :], (H, n_dr, W, 2 * W - 1))
    ext = jnp.pad(ext, ((0, 0), (0, 0), (0, 0), (0, 1))).reshape(H, n_dr, W * 2 * W)
    skew = ext[:, :, W - 1:W - 1 + W * (2 * W - 1)].reshape(H, n_dr, W, 2 * W - 1)[..., :W]
    col = np.arange(W)
    col_start = np.clip(col - NA_WIN_COLS // 2, 0, W - NA_WIN_COLS)
    col_in = (col[None, :] >= col_start[:, None]) & (col[None, :] < col_start[:, None] + NA_WIN_COLS)
    col_bias = jnp.where(col_in[None, None], skew, MASK_BIAS)
    col_bias = jnp.concatenate([col_bias, jnp.full((H, 1, W, W), MASK_BIAS, F32)], axis=1)
    n_tiles = n_rows // NA_TILE_ROWS
    tiles = np.array([0, 1, n_tiles - 1])
    ks = np.clip(NA_TILE_ROWS * tiles - NA_WIN_ROWS // 2, 0, n_rows - NA_KEY_ROWS)
    qrow = NA_TILE_ROWS * tiles[:, None] + np.arange(NA_TILE_ROWS)[None]
    krow = ks[:, None] + np.arange(NA_KEY_ROWS)[None]
    r0 = np.clip(qrow - NA_WIN_ROWS // 2, 0, n_rows - NA_WIN_ROWS)
    row_in = (krow[:, None, :] >= r0[:, :, None]) & (krow[:, None, :] < r0[:, :, None] + NA_WIN_ROWS)
    row_idx = np.where(row_in, krow[:, None, :] - qrow[:, :, None] + NA_WIN_ROWS - 1, n_dr)
    blocks = [jnp.concatenate([col_bias[:, int(row_idx[v, j, r])] for r in range(NA_KEY_ROWS)], axis=-1)
              for v in range(3) for j in range(NA_TILE_ROWS)]
    return jnp.stack(blocks, axis=1).reshape(H, 3, TQ, NA_KEY_ROWS * W)


def _na_attention(qkv, bias, n_ctx):
    B, T, W = qkv.shape
    H = W // (3 * HEAD_V)
    nq = T // TQ
    n_ctx_tiles = n_ctx // TQ
    n_rows = (T - n_ctx) // GRID_W
    n_lat_tiles = nq - n_ctx_tiles

    def bias_index(b, h, i):
        t = i - n_ctx_tiles
        return (h, jnp.where(t <= 0, 0, jnp.where(t == n_lat_tiles - 1, 2, 1)), 0, 0)

    return pl.pallas_call(
        functools.partial(_na_body, n_ctx_tiles=n_ctx_tiles, n_ctx=n_ctx, n_rows=n_rows),
        grid=(B, H, nq),
        in_specs=[pl.BlockSpec((1, TQ, HEAD_V), lambda b, h, i: (b, i, h)),
                  pl.BlockSpec((1, T, HEAD_V), lambda b, h, i: (b, 0, H + h)),
                  pl.BlockSpec((1, T, HEAD_V), lambda b, h, i: (b, 0, 2 * H + h)),
                  pl.BlockSpec((1, 1, TQ, NA_KEY_ROWS * GRID_W), bias_index)],
        out_specs=pl.BlockSpec((1, TQ, HEAD_V), lambda b, h, i: (b, i, h)),
        out_shape=jax.ShapeDtypeStruct((B, T, H * HEAD_V), BF16),
        compiler_params=_params("parallel", "parallel", "arbitrary"),
        name="neighbourhood_attention",
    )(qkv, qkv, qkv, bias)


def _rope_tables(n_ctx, seq):
    pos = jnp.arange(seq)
    quarter = DA_QK_DIM // 4
    inv = ROPE_BASE ** (-jnp.arange(quarter, dtype=F32) / quarter)
    ang_r = (pos // GRID_W).astype(F32)[:, None] * inv
    ang_c = (pos % GRID_W).astype(F32)[:, None] * inv
    cos = jnp.concatenate([jnp.cos(ang_r)] * 2 + [jnp.cos(ang_c)] * 2, axis=1)
    sin = jnp.concatenate([-jnp.sin(ang_r), jnp.sin(ang_r), -jnp.sin(ang_c), jnp.sin(ang_c)], axis=1)
    cos = jnp.concatenate([jnp.ones((n_ctx, DA_QK_DIM), F32), cos], axis=0)
    sin = jnp.concatenate([jnp.zeros((n_ctx, DA_QK_DIM), F32), sin], axis=0)
    cos = jnp.concatenate([cos, cos], axis=1)
    sin = jnp.concatenate([sin, sin], axis=1)
    return jnp.stack([cos, jnp.ones_like(cos)]), jnp.stack([sin, jnp.zeros_like(sin)])


def kernel(x, c, ctx, c_ctx, ada_w1, ada_w2, ada_b, ln_g, ln_b, mlp_w1, mlp_w2, da_w_qkv, da_w_o, da_lq1,
           da_lk1, da_lq2, da_lk2, da_sub_g, mla_w_down, mla_q_g, mla_w_uq, mla_kv_g, mla_w_ukv, mla_w_o,
           na_w_qkv, na_w_o, na_rpb):
    B, S, D = x.shape
    C = ctx.shape[1]
    T = C + S
    M = B * T
    depth = ada_w1.shape[0]
    alpha = (2.0 * depth) ** 0.25
    n_ctx_tiles = C // ROW_TILE
    assert C % TQ == 0 and S % KEY_CHUNK == 0 and ROW_TILE == TQ and B + 1 <= 8

    cos, sin = _rope_tables(C, S)
    xs = jnp.concatenate([ctx, x], axis=1)
    cond = jnp.concatenate([c, c_ctx[None], jnp.zeros((8 - B - 1, D), F32)], axis=0)
    mods = [_modulation(cond, ada_w1, ada_w2, ada_b, i) for i in range(depth)]

    h = _modulate(xs, mods[0], n_ctx_tiles, scale_idx=1, shift_idx=0)
    out = None
    for i in range(depth):
        kind, slot = i % N_MIXERS, i // N_MIXERS
        h2 = h.reshape(M, D)
        if kind == 0:
            lam_init = 0.8 - 0.6 * math.exp(-0.3 * i)
            n_heads = da_w_o.shape[1] // HEAD_V
            qk_cols = 2 * n_heads * HEAD_V
            qkv = _matmul(h2, da_w_qkv, slot, BF16, rope=(cos, sin, T, 0, qk_cols), name="da_qkv")
            lam_params = jnp.zeros((8, LANES), F32).at[:4, :DA_QK_DIM].set(
                jnp.stack([da_lq1[slot], da_lk1[slot], da_lq2[slot], da_lk2[slot]]))
            o = _diff_attention(qkv.reshape(B, T, -1), lam_params, da_sub_g[slot][None], C, lam_init)
            w_o = da_w_o
        elif kind == 1:
            w_down = mla_w_down[slot]
            q_rank = mla_q_g.shape[1]
            kv_rank = mla_kv_g.shape[1]
            n_heads = mla_w_o.shape[1] // HEAD_V
            k_rope_w = w_down[:, q_rank + kv_rank:]
            w_down = jnp.concatenate([w_down, k_rope_w, jnp.zeros((D, LANES), F32)], axis=1)
            down = _matmul(h2, w_down[None], 0, F32, name="mla_down")
            cq, ckv, kr = _mla_norm(down.reshape(B, T, -1), mla_q_g[slot][None], mla_kv_g[slot][None], cos, sin)
            w_uq = mla_w_uq[slot].reshape(q_rank, n_heads, MLA_NOPE + MLA_ROPE)
            w_uq = jnp.concatenate([w_uq[:, :, :MLA_NOPE].reshape(q_rank, -1),
                                    w_uq[:, :, MLA_NOPE:].reshape(q_rank, -1)], axis=1)
            nope_cols = n_heads * MLA_NOPE
            q = _matmul(cq.reshape(M, q_rank), w_uq[None], 0, BF16,
                        rope=(cos, sin, T, nope_cols, nope_cols + n_heads * MLA_ROPE), name="mla_uq")
            w_ukv = mla_w_ukv[slot].reshape(kv_rank, n_heads, MLA_NOPE + HEAD_V)
            w_ukv = jnp.concatenate([w_ukv[:, :, :MLA_NOPE].reshape(kv_rank, -1),
                                     w_ukv[:, :, MLA_NOPE:].reshape(kv_rank, -1)], axis=1)
            kv = _matmul(ckv.reshape(M, kv_rank), w_ukv[None], 0, BF16, name="mla_ukv")
            o = _mla_attention(q.reshape(B, T, -1), kv.reshape(B, T, -1), kr, C)
            w_o = mla_w_o
        else:
            qkv = _matmul(h2, na_w_qkv, slot, BF16, name="na_qkv")
            bias = _na_bias_table(na_rpb[slot], S // GRID_W)
            o = _na_attention(qkv.reshape(B, T, -1), bias, C)
            w_o = na_w_o
        y = _matmul(o.reshape(M, -1), w_o, slot, F32, name="attn_out").reshape(B, T, D)
        xs, h = _ln_mod(xs, y, mods[i], ln_g, ln_b, mods[i], n_ctx_tiles, layer=i, alpha=alpha, gate_idx=2,
                        ln_idx=0, scale_idx=4, shift_idx=3)
        u = _matmul(h.reshape(M, D), mlp_w1, i, BF16, epilogue=_epi_relu2, name="mlp_up")
        y = _matmul(u, mlp_w2, i, F32, name="mlp_down").reshape(B, T, D)
        if i + 1 < depth:
            xs, h = _ln_mod(xs, y, mods[i], ln_g, ln_b, mods[i + 1], n_ctx_tiles, layer=i, alpha=alpha,
                            gate_idx=5, ln_idx=1, scale_idx=1, shift_idx=0)
        else:
            out, _ = _ln_mod(xs, y, mods[i], ln_g, ln_b, None, n_ctx_tiles, layer=i, alpha=alpha, gate_idx=5,
                             ln_idx=1, latent_only=True)
    return out
```

```python
import functools
import math

import numpy as np
import jax
import jax.numpy as jnp
from jax import lax
from jax.experimental import pallas as pl
from jax.experimental.pallas import tpu as pltpu

F32 = jnp.float32
BF16 = jnp.bfloat16

GRID_W = 64
N_MOD = 6
ROPE_BASE = 10000.0
LN_EPS = 1e-5
RMS_EPS = 1e-6
DA_SUB_EPS = 1e-5
DA_QK_DIM = 64
HEAD_V = 128
MLA_NOPE = 128
MLA_ROPE = 64
NA_WIN_ROWS = 8
NA_WIN_COLS = 16
N_MIXERS = 3
MASK_BIAS = -1e30
LOG2E = 1.4426950408889634

LANES = 128
MXU_DIM = 256
VMEM_LIMIT_BYTES = 56 * 1024 * 1024

TQ = 256
KEY_CHUNK = 4 * MXU_DIM
FLASH_LOOKAHEAD = 4
DA_HEADS_PER_STEP = 2
MLA_HEADS_PER_STEP = 4
NA_HEADS_PER_STEP = 4
NA_TILE_ROWS = TQ // GRID_W
NA_KEY_ROWS = 12
ROW_TILE = 256
MM_TM = 1088
MM_TN = 512
MM_TK_FULL = 4096
MM_TK = 2048
MM_ROW_SPLITS = 4


def _pick_tile(dim, target, align):
    best = None
    for t in range(align, min(dim, target) + 1, align):
        if dim % t == 0:
            best = t
    return best if best is not None else dim


def _params(*sem):
    return pltpu.CompilerParams(dimension_semantics=sem, vmem_limit_bytes=VMEM_LIMIT_BYTES)


def _rope(x, cos, sin):
    n = x.shape[1]
    reps = n // LANES
    if reps > 1:
        cos = jnp.concatenate([cos] * reps, axis=1)
        sin = jnp.concatenate([sin] * reps, axis=1)
    lane = lax.broadcasted_iota(jnp.int32, x.shape, 1)
    nxt = pltpu.roll(x, n - 16, 1)
    prv = pltpu.roll(x, 16, 1)
    partner = jnp.where((lane & 16) == 0, nxt, prv)
    return x * cos + partner * sin


def _mm_body(*refs, nk, epilogue, n_extra, splits):
    x_ref, w_ref = refs[0], refs[1]
    extra = refs[2:2 + n_extra]
    o_ref = refs[2 + n_extra]
    wb_ref = refs[3 + n_extra]
    k = pl.program_id(1)
    i = pl.program_id(2)

    @pl.when(i == 0)
    def _():
        wb_ref[...] = w_ref[...].astype(BF16)

    if nk > 1:
        acc_ref = refs[4 + n_extra]

        @pl.when((pl.program_id(0) == 0) & (k == 0))
        def _():
            acc_ref[i] = jnp.zeros(acc_ref.shape[1:], F32)

    tm = x_ref.shape[0]
    sub = tm // splits
    for r in range(splits):
        rows = slice(r * sub, (r + 1) * sub)
        acc = jnp.dot(x_ref[rows, :], wb_ref[...], preferred_element_type=F32)
        if nk > 1:
            acc = jnp.where(k > 0, acc_ref[i, rows, :], 0.0) + acc
            acc_ref[i, rows, :] = acc
        epilogue(acc, rows, o_ref, *extra)


def _epi_identity(acc, rows, o_ref):
    o_ref[rows, :] = acc.astype(o_ref.dtype)


def _epi_relu2(acc, rows, o_ref):
    r = jnp.maximum(acc, 0.0)
    o_ref[rows, :] = (r * r).astype(o_ref.dtype)


def _epi_rope(acc, rows, o_ref, cos_ref, sin_ref):
    o_ref[rows, :] = _rope(acc, cos_ref[rows, :], sin_ref[rows, :]).astype(o_ref.dtype)


def _matmul(x, w, layer, out_dtype, *, epilogue=_epi_identity, rope=None, name="matmul"):
    M, K = x.shape
    N = w.shape[2]
    tk = K if K <= MM_TK_FULL else _pick_tile(K, MM_TK, LANES)
    nk = K // tk
    tm_target, tn_target = (MM_TM, MM_TN) if nk > 1 else (MM_TM // 2, 2 * MM_TN)
    if rope is None:
        tm = _pick_tile(M, tm_target, 16)
        tn = _pick_tile(N, tn_target, LANES)
    else:
        tm = _pick_tile(rope[2], tm_target, 16)
        tn = _pick_tile(math.gcd(N, rope[3], rope[4]), tn_target, LANES)
    nm = M // tm
    last = nk - 1
    in_specs = [pl.BlockSpec((tm, tk), lambda j, k, i: (i, k)),
                pl.BlockSpec((None, tk, tn), lambda j, k, i: (layer, k, j))]
    operands = [x, w]
    n_extra = 0
    if rope is not None:
        cos, sin, rows_per_batch, lo, hi = rope
        per = rows_per_batch // tm
        tab_spec = pl.BlockSpec(
            (None, tm, LANES),
            lambda j, k, i: (jnp.where((j * tn >= lo) & (j * tn < hi), 0, 1), i % per, 0))
        in_specs += [tab_spec, tab_spec]
        operands += [cos, sin]
        n_extra = 2
        epilogue = _epi_rope
    scratch = [pltpu.VMEM((tk, tn), BF16)]
    if nk > 1:
        scratch.append(pltpu.VMEM((nm, tm, tn), F32))
    out_spec = pl.BlockSpec((tm, tn), lambda j, k, i: (jnp.where(k == last, i, 0), j))
    return pl.pallas_call(
        functools.partial(_mm_body, nk=nk, epilogue=epilogue, n_extra=n_extra,
                          splits=max(s for s in (MM_ROW_SPLITS, 2, 1) if tm % (16 * s) == 0)),
        grid=(N // tn, nk, nm),
        in_specs=in_specs,
        out_specs=out_spec,
        out_shape=jax.ShapeDtypeStruct((M, N), out_dtype),
        scratch_shapes=scratch,
        compiler_params=_params("parallel", "arbitrary", "arbitrary"),
        name=name,
    )(*operands)


def _mod_body(c_ref, w1_ref, w2_ref, b_ref, o_ref):
    c = c_ref[...]
    s = c * (1.0 / (1.0 + jnp.exp(-c)))
    h = jnp.dot(s.astype(BF16), w1_ref[...].astype(BF16), preferred_element_type=F32)
    o = jnp.dot(h.astype(BF16), w2_ref[...].astype(BF16), preferred_element_type=F32)
    o_ref[...] = o + b_ref[...]


def _modulation(cond, w1, w2, b, layer):
    R, D = cond.shape
    rank = w1.shape[2]
    N = w2.shape[2]
    tn = _pick_tile(N, 2048, LANES)
    out = pl.pallas_call(
        _mod_body,
        grid=(N // tn,),
        in_specs=[pl.BlockSpec((R, D), lambda j: (0, 0)),
                  pl.BlockSpec((None, D, rank), lambda j: (layer, 0, 0)),
                  pl.BlockSpec((None, rank, tn), lambda j: (layer, 0, j)),
                  pl.BlockSpec((None, 1, tn), lambda j: (layer, 0, j))],
        out_specs=pl.BlockSpec((R, tn), lambda j: (0, j)),
        out_shape=jax.ShapeDtypeStruct((R, N), F32),
        compiler_params=_params("arbitrary"),
        name="adaln_modulation",
    )(cond, w1, w2, b.reshape(b.shape[0], 1, N))
    return out.reshape(R, N_MOD, D)


def _modulate_body(x_ref, mod_ref, h_ref, *, scale_idx, shift_idx):
    x = x_ref[0]
    scale = mod_ref[0, scale_idx:scale_idx + 1, :]
    shift = mod_ref[0, shift_idx:shift_idx + 1, :]
    h_ref[0] = (x * (1.0 + scale) + shift).astype(h_ref.dtype)


def _modulate(x, mod, n_ctx_tiles, *, scale_idx, shift_idx):
    B, T, D = x.shape
    ctx_row = B
    return pl.pallas_call(
        functools.partial(_modulate_body, scale_idx=scale_idx, shift_idx=shift_idx),
        grid=(B, T // ROW_TILE),
        in_specs=[pl.BlockSpec((1, ROW_TILE, D), lambda b, t: (b, t, 0)),
                  pl.BlockSpec((1, N_MOD, D), lambda b, t: (jnp.where(t < n_ctx_tiles, ctx_row, b), 0, 0))],
        out_specs=pl.BlockSpec((1, ROW_TILE, D), lambda b, t: (b, t, 0)),
        out_shape=jax.ShapeDtypeStruct((B, T, D), BF16),
        compiler_params=_params("parallel", "parallel"),
        name="modulate",
    )(x, mod)


def _ln_mod_body(x_ref, y_ref, modg_ref, lng_ref, lnb_ref, *rest, alpha, gate_idx, ln_idx, scale_idx,
                 shift_idx, want_h):
    if want_h:
        modn_ref, xo_ref, ho_ref = rest
    else:
        (xo_ref,) = rest
    x = x_ref[0]
    y = y_ref[0]
    gate = modg_ref[0, gate_idx:gate_idx + 1, :]
    z = alpha * x + gate * y
    mu = jnp.mean(z, axis=-1, keepdims=True)
    zc = z - mu
    var = jnp.mean(zc * zc, axis=-1, keepdims=True)
    xn = (zc * lax.rsqrt(var + LN_EPS)) * lng_ref[ln_idx:ln_idx + 1, :] + lnb_ref[ln_idx:ln_idx + 1, :]
    xo_ref[0] = xn
    if want_h:
        scale = modn_ref[0, scale_idx:scale_idx + 1, :]
        shift = modn_ref[0, shift_idx:shift_idx + 1, :]
        ho_ref[0] = (xn * (1.0 + scale) + shift).astype(ho_ref.dtype)


def _ln_mod(x, y, mod_gate, ln_g, ln_b, mod_next, n_ctx_tiles, *, layer, alpha, gate_idx, ln_idx, scale_idx=0,
            shift_idx=0, latent_only=False):
    B, T, D = x.shape
    ctx_row = B
    want_h = mod_next is not None
    off = n_ctx_tiles if latent_only else 0
    n_tiles = T // ROW_TILE - off
    row_in = pl.BlockSpec((1, ROW_TILE, D), lambda b, t: (b, t + off, 0))
    row_out = pl.BlockSpec((1, ROW_TILE, D), lambda b, t: (b, t, 0))
    mod_spec = pl.BlockSpec((1, N_MOD, D), lambda b, t: (jnp.where(t + off < n_ctx_tiles, ctx_row, b), 0, 0))
    ln_spec = pl.BlockSpec((None, 2, D), lambda b, t: (layer, 0, 0))
    in_specs = [row_in, row_in, mod_spec, ln_spec, ln_spec]
    operands = [x, y, mod_gate, ln_g, ln_b]
    out_specs = [row_out]
    out_shape = [jax.ShapeDtypeStruct((B, n_tiles * ROW_TILE, D), F32)]
    if want_h:
        in_specs.append(mod_spec)
        operands.append(mod_next)
        out_specs.append(row_out)
        out_shape.append(jax.ShapeDtypeStruct((B, n_tiles * ROW_TILE, D), BF16))
    res = pl.pallas_call(
        functools.partial(_ln_mod_body, alpha=alpha, gate_idx=gate_idx, ln_idx=ln_idx, scale_idx=scale_idx,
                          shift_idx=shift_idx, want_h=want_h),
        grid=(B, n_tiles),
        in_specs=in_specs,
        out_specs=out_specs,
        out_shape=out_shape,
        compiler_params=_params("parallel", "parallel"),
        name="resid_ln_modulate",
    )(*operands)
    return (res[0], res[1]) if want_h else (res[0], None)


def _flash(items, s_ref, scale, pre_scaled=False):
    c = LOG2E if pre_scaled else scale * LOG2E

    def put(i):
        it = items[i]
        s = lax.dot_general(it["k"](), it["q"], (((1,), (1,)), ((), ())), preferred_element_type=F32)
        s_ref[i % FLASH_LOOKAHEAD, :it["n"], :] = s

    def pv(it, p):
        return lax.dot_general(it["v"](), p.astype(BF16), (((0,), (0,)), ((), ())), preferred_element_type=F32)

    for i in range(min(FLASH_LOOKAHEAD, len(items))):
        put(i)
    state = None
    for idx, it in enumerate(items):
        s = s_ref[idx % FLASH_LOOKAHEAD, :it["n"], :]
        if idx + FLASH_LOOKAHEAD < len(items):
            put(idx + FLASH_LOOKAHEAD)
        if pre_scaled:
            s = s * scale
            if it.get("bias") is not None:
                s = s + it["bias"]()
        cm = jnp.max(s, axis=0, keepdims=True)
        if it["first"]:
            m = cm
            p = jnp.exp2((s - m) * c)
            l = jnp.sum(p, axis=0, keepdims=True)
            acc = pv(it, p)
        else:
            m_old, l_old, acc_old = state
            m = jnp.maximum(m_old, cm)
            alpha = jnp.exp2((m_old - m) * c)
            p = jnp.exp2((s - m) * c)
            l = alpha * l_old + jnp.sum(p, axis=0, keepdims=True)
            acc = alpha * acc_old + pv(it, p)
        state = (m, l, acc)
        if it["last"]:
            it["emit"]((acc / l).T)


def _score_scratch():
    return pltpu.VMEM((FLASH_LOOKAHEAD, KEY_CHUNK, TQ), F32)


def _key_chunks(n_ctx, n_keys):
    return [(0, n_ctx)] + [(lo, KEY_CHUNK) for lo in range(n_ctx, n_keys, KEY_CHUNK)]


def _problem_items(q, k_get, v_get, chunks, emit):
    return [dict(q=q, k=functools.partial(k_get, lo, n), v=functools.partial(v_get, lo, n), n=n,
                 first=ci == 0, last=ci == len(chunks) - 1, emit=emit)
            for ci, (lo, n) in enumerate(chunks)]


def _for_ctx_and_latent_tiles(qi, n_ctx_tiles, n_ctx, n_keys, run):
    @pl.when(qi < n_ctx_tiles)
    def _():
        run(_key_chunks(n_ctx, n_ctx))

    @pl.when(qi >= n_ctx_tiles)
    def _():
        run(_key_chunks(n_ctx, n_keys))


def _da_body(lam_ref, q_ref, k_ref, v_ref, g_ref, o_ref, s_ref, *, heads, n_ctx_tiles, n_ctx, lam_init):
    lp = lam_ref[...]
    lam = (jnp.exp(jnp.sum(lp[0:1] * lp[1:2], axis=-1, keepdims=True))
           - jnp.exp(jnp.sum(lp[2:3] * lp[3:4], axis=-1, keepdims=True)) + lam_init)
    lane = lax.broadcasted_iota(jnp.int32, (TQ, HEAD_V), 1)

    def run(chunks):
        items = []
        for j in range(heads):
            cols = slice(j * HEAD_V, (j + 1) * HEAD_V)
            q = q_ref[0, :, cols]
            zero = jnp.zeros_like(q)
            comps = [jnp.where(lane < DA_QK_DIM, q, zero), jnp.where(lane >= DA_QK_DIM, q, zero)]
            outs = []

            def finish(o1, cols=cols, outs=outs):
                o = outs[0] - lam * o1
                ms = jnp.mean(o * o, axis=-1, keepdims=True)
                on = (o * lax.rsqrt(ms + DA_SUB_EPS)) * g_ref[...] * (1.0 - lam_init)
                o_ref[0, :, cols] = on.astype(o_ref.dtype)

            def k_get(lo, n, cols=cols):
                return k_ref[0, lo:lo + n, cols]

            def v_get(lo, n, cols=cols):
                return v_ref[0, lo:lo + n, cols]

            items += _problem_items(comps[0], k_get, v_get, chunks, outs.append)
            items += _problem_items(comps[1], k_get, v_get, chunks, finish)
        _flash(items, s_ref, DA_QK_DIM ** -0.5)

    _for_ctx_and_latent_tiles(pl.program_id(2), n_ctx_tiles, n_ctx, k_ref.shape[1], run)


def _diff_attention(qkv, lam_params, sub_g, n_ctx, lam_init):
    B, T, W3 = qkv.shape
    H = W3 // (3 * HEAD_V)
    G = math.gcd(H, DA_HEADS_PER_STEP)
    W = G * HEAD_V
    nq = T // TQ
    return pl.pallas_call(
        functools.partial(_da_body, heads=G, n_ctx_tiles=n_ctx // TQ, n_ctx=n_ctx, lam_init=lam_init),
        grid=(B, H // G, nq),
        in_specs=[pl.BlockSpec((8, LANES), lambda b, g, i: (0, 0)),
                  pl.BlockSpec((1, TQ, W), lambda b, g, i: (b, i, g)),
                  pl.BlockSpec((1, T, W), lambda b, g, i: (b, 0, H // G + g)),
                  pl.BlockSpec((1, T, W), lambda b, g, i: (b, 0, 2 * (H // G) + g)),
                  pl.BlockSpec((1, HEAD_V), lambda b, g, i: (0, 0))],
        out_specs=pl.BlockSpec((1, TQ, W), lambda b, g, i: (b, i, g)),
        out_shape=jax.ShapeDtypeStruct((B, T, H * HEAD_V), BF16),
        scratch_shapes=[_score_scratch()],
        compiler_params=_params("parallel", "parallel", "arbitrary"),
        name="diff_attention",
    )(lam_params, qkv, qkv, qkv, sub_g)


def _mla_norm_body(d_ref, qg_ref, kvg_ref, cos_ref, sin_ref, cq_ref, ckv_ref, kr_ref, *, q_rank, kv_rank):
    d = d_ref[0]

    def rms(x, g):
        ms = jnp.mean(x * x, axis=-1, keepdims=True)
        return (x * lax.rsqrt(ms + RMS_EPS)) * g

    cq_ref[0] = rms(d[:, :q_rank], qg_ref[...]).astype(cq_ref.dtype)
    ckv_ref[0] = rms(d[:, q_rank:q_rank + kv_rank], kvg_ref[...]).astype(ckv_ref.dtype)
    kr = d[:, q_rank + kv_rank:q_rank + kv_rank + LANES]
    kr_ref[0] = _rope(kr, cos_ref[...], sin_ref[...]).astype(kr_ref.dtype)


def _mla_norm(down, q_g, kv_g, cos, sin):
    B, T, W = down.shape
    q_rank = q_g.shape[1]
    kv_rank = kv_g.shape[1]
    row = lambda w: pl.BlockSpec((1, ROW_TILE, w), lambda b, t: (b, t, 0))
    full = lambda w: pl.BlockSpec((1, w), lambda b, t: (0, 0))
    tab = pl.BlockSpec((None, ROW_TILE, LANES), lambda b, t: (0, t, 0))
    return pl.pallas_call(
        functools.partial(_mla_norm_body, q_rank=q_rank, kv_rank=kv_rank),
        grid=(B, T // ROW_TILE),
        in_specs=[row(W), full(q_rank), full(kv_rank), tab, tab],
        out_specs=[row(q_rank), row(kv_rank), row(LANES)],
        out_shape=[jax.ShapeDtypeStruct((B, T, q_rank), BF16),
                   jax.ShapeDtypeStruct((B, T, kv_rank), BF16),
                   jax.ShapeDtypeStruct((B, T, LANES), BF16)],
        compiler_params=_params("parallel", "parallel"),
        name="mla_norm_rope",
    )(down, q_g, kv_g, cos, sin)


def _mla_body(qn_ref, qr_ref, kn_ref, kr_ref, v_ref, o_ref, s_ref, *, heads, n_ctx_tiles, n_ctx):
    lane = lax.broadcasted_iota(jnp.int32, (TQ, LANES), 1)

    def run(chunks):
        items = []
        for j in range(heads):
            cols = slice(j * HEAD_V, (j + 1) * HEAD_V)
            pair = slice((j // 2) * LANES, (j // 2 + 1) * LANES)
            qr = qr_ref[0, :, pair]
            mine = (lane // MLA_ROPE) == (j % 2)
            q = jnp.concatenate([qn_ref[0, :, cols], jnp.where(mine, qr, jnp.zeros_like(qr))], axis=1)

            def k_get(lo, n, cols=cols):
                return jnp.concatenate([kn_ref[0, lo:lo + n, cols], kr_ref[0, lo:lo + n, :]], axis=1)

            def v_get(lo, n, cols=cols):
                return v_ref[0, lo:lo + n, cols]

            def emit(o, cols=cols):
                o_ref[0, :, cols] = o.astype(o_ref.dtype)

            items += _problem_items(q, k_get, v_get, chunks, emit)
        _flash(items, s_ref, (MLA_NOPE + MLA_ROPE) ** -0.5)

    _for_ctx_and_latent_tiles(pl.program_id(2), n_ctx_tiles, n_ctx, kn_ref.shape[1], run)


def _mla_attention(q, kv, kr, n_ctx):
    B, T, Wq = q.shape
    H = Wq // (MLA_NOPE + MLA_ROPE)
    nq = T // TQ
    G = math.gcd(H, MLA_HEADS_PER_STEP)
    assert G % 2 == 0
    W = G * HEAD_V
    Wr = G * MLA_ROPE
    return pl.pallas_call(
        functools.partial(_mla_body, heads=G, n_ctx_tiles=n_ctx // TQ, n_ctx=n_ctx),
        grid=(B, H // G, nq),
        in_specs=[pl.BlockSpec((1, TQ, W), lambda b, g, i: (b, i, g)),
                  pl.BlockSpec((1, TQ, Wr), lambda b, g, i: (b, i, (H * MLA_NOPE) // Wr + g)),
                  pl.BlockSpec((1, T, W), lambda b, g, i: (b, 0, g)),
                  pl.BlockSpec((1, T, LANES), lambda b, g, i: (b, 0, 0)),
                  pl.BlockSpec((1, T, W), lambda b, g, i: (b, 0, H // G + g))],
        out_specs=pl.BlockSpec((1, TQ, W), lambda b, g, i: (b, i, g)),
        out_shape=jax.ShapeDtypeStruct((B, T, H * HEAD_V), BF16),
        scratch_shapes=[_score_scratch()],
        compiler_params=_params("parallel", "parallel", "arbitrary"),
        name="mla_attention",
    )(q, q, kv, kr, kv)


def _na_key_row_start(i, n_rows):
    return jnp.clip(NA_TILE_ROWS * i - NA_WIN_ROWS // 2, 0, n_rows - NA_KEY_ROWS)


def _na_body(q_ref, k_ref, v_ref, bias_ref, o_ref, s_ref, *, heads, n_ctx_tiles, n_ctx, n_rows):
    qi = pl.program_id(2)
    n_lat = NA_KEY_ROWS * GRID_W

    def run(latent):
        start = pl.multiple_of(n_ctx + _na_key_row_start(qi - n_ctx_tiles, n_rows) * GRID_W, GRID_W)
        items = []
        for j in range(heads):
            cols = slice(j * HEAD_V, (j + 1) * HEAD_V)

            def emit(o, cols=cols):
                o_ref[0, :, cols] = o.astype(o_ref.dtype)

            ctx_item = dict(q=q_ref[0, :, cols], n=n_ctx, first=True, last=not latent, emit=emit,
                            k=lambda cols=cols: k_ref[0, :n_ctx, cols],
                            v=lambda cols=cols: v_ref[0, :n_ctx, cols])
            items.append(ctx_item)
            if latent:
                items.append(dict(q=q_ref[0, :, cols], n=n_lat, first=False, last=True, emit=emit,
                                  k=lambda cols=cols: k_ref[0, pl.ds(start, n_lat), cols],
                                  v=lambda cols=cols: v_ref[0, pl.ds(start, n_lat), cols],
                                  bias=lambda j=j: bias_ref[j]))
        _flash(items, s_ref, HEAD_V ** -0.5, pre_scaled=True)

    @pl.when(qi < n_ctx_tiles)
    def _():
        run(False)

    @pl.when(qi >= n_ctx_tiles)
    def _():
        run(True)


def _na_bias_table(rpb, n_rows):
    H = rpb.shape[0]
    n_dr = 2 * NA_WIN_ROWS - 1
    W = GRID_W
    off_idx = np.clip(np.arange(2 * W - 1) - (W - 1) + NA_WIN_COLS - 1, 0, 2 * NA_WIN_COLS - 2)
    ext = rpb[:, :, off_idx]
    ext = jnp.broadcast_to(ext[:, :, None, :], (H, n_dr, W, 2 * W - 1))
    ext = jnp.pad(ext, ((0, 0), (0, 0), (0, 0), (0, 1))).reshape(H, n_dr, W * 2 * W)
    skew = ext[:, :, W - 1:W - 1 + W * (2 * W - 1)].reshape(H, n_dr, W, 2 * W - 1)[..., :W]
    col = np.arange(W)
    col_start = np.clip(col - NA_WIN_COLS // 2, 0, W - NA_WIN_COLS)
    col_in = (col[None, :] >= col_start[:, None]) & (col[None, :] < col_start[:, None] + NA_WIN_COLS)
    col_bias = jnp.where(col_in[None, None], skew, MASK_BIAS)
    col_bias = jnp.concatenate([col_bias, jnp.full((H, 1, W, W), MASK_BIAS, F32)], axis=1)
    n_tiles = n_rows // NA_TILE_ROWS
    tiles = np.array([0, 1, n_tiles - 1])
    ks = np.clip(NA_TILE_ROWS * tiles - NA_WIN_ROWS // 2, 0, n_rows - NA_KEY_ROWS)
    qrow = NA_TILE_ROWS * tiles[:, None] + np.arange(NA_TILE_ROWS)[None]
    krow = ks[:, None] + np.arange(NA_KEY_ROWS)[None]
    r0 = np.clip(qrow - NA_WIN_ROWS // 2, 0, n_rows - NA_WIN_ROWS)
    row_in = (krow[:, None, :] >= r0[:, :, None]) & (krow[:, None, :] < r0[:, :, None] + NA_WIN_ROWS)
    row_idx = np.where(row_in, krow[:, None, :] - qrow[:, :, None] + NA_WIN_ROWS - 1, n_dr)
    col_bias_t = jnp.swapaxes(col_bias, 2, 3)
    blocks = [jnp.concatenate([col_bias_t[:, int(row_idx[v, j, r])] for j in range(NA_TILE_ROWS)], axis=-1)
              for v in range(3) for r in range(NA_KEY_ROWS)]
    return jnp.stack(blocks, axis=1).reshape(H, 3, NA_KEY_ROWS * W, TQ)


def _na_attention(qkv, bias, n_ctx):
    B, T, W3 = qkv.shape
    H = W3 // (3 * HEAD_V)
    G = math.gcd(H, NA_HEADS_PER_STEP)
    W = G * HEAD_V
    nq = T // TQ
    n_ctx_tiles = n_ctx // TQ
    n_rows = (T - n_ctx) // GRID_W
    n_lat_tiles = nq - n_ctx_tiles

    def bias_index(b, g, i):
        t = i - n_ctx_tiles
        return (g, jnp.where(t <= 0, 0, jnp.where(t == n_lat_tiles - 1, 2, 1)), 0, 0)

    return pl.pallas_call(
        functools.partial(_na_body, heads=G, n_ctx_tiles=n_ctx_tiles, n_ctx=n_ctx, n_rows=n_rows),
        grid=(B, H // G, nq),
        in_specs=[pl.BlockSpec((1, TQ, W), lambda b, g, i: (b, i, g)),
                  pl.BlockSpec((1, T, W), lambda b, g, i: (b, 0, H // G + g)),
                  pl.BlockSpec((1, T, W), lambda b, g, i: (b, 0, 2 * (H // G) + g)),
                  pl.BlockSpec((G, None, NA_KEY_ROWS * GRID_W, TQ), bias_index)],
        out_specs=pl.BlockSpec((1, TQ, W), lambda b, g, i: (b, i, g)),
        out_shape=jax.ShapeDtypeStruct((B, T, H * HEAD_V), BF16),
        scratch_shapes=[_score_scratch()],
        compiler_params=_params("parallel", "parallel", "arbitrary"),
        name="neighbourhood_attention",
    )(qkv, qkv, qkv, bias)


def _rope_tables(n_ctx, seq):
    pos = jnp.arange(seq)
    quarter = DA_QK_DIM // 4
    inv = ROPE_BASE ** (-jnp.arange(quarter, dtype=F32) / quarter)
    ang_r = (pos // GRID_W).astype(F32)[:, None] * inv
    ang_c = (pos % GRID_W).astype(F32)[:, None] * inv
    cos = jnp.concatenate([jnp.cos(ang_r)] * 2 + [jnp.cos(ang_c)] * 2, axis=1)
    sin = jnp.concatenate([-jnp.sin(ang_r), jnp.sin(ang_r), -jnp.sin(ang_c), jnp.sin(ang_c)], axis=1)
    cos = jnp.concatenate([jnp.ones((n_ctx, DA_QK_DIM), F32), cos], axis=0)
    sin = jnp.concatenate([jnp.zeros((n_ctx, DA_QK_DIM), F32), sin], axis=0)
    cos = jnp.concatenate([cos, cos], axis=1)
    sin = jnp.concatenate([sin, sin], axis=1)
    return jnp.stack([cos, jnp.ones_like(cos)]), jnp.stack([sin, jnp.zeros_like(sin)])


def kernel(x, c, ctx, c_ctx, ada_w1, ada_w2, ada_b, ln_g, ln_b, mlp_w1, mlp_w2, da_w_qkv, da_w_o, da_lq1,
           da_lk1, da_lq2, da_lk2, da_sub_g, mla_w_down, mla_q_g, mla_w_uq, mla_kv_g, mla_w_ukv, mla_w_o,
           na_w_qkv, na_w_o, na_rpb):
    B, S, D = x.shape
    C = ctx.shape[1]
    T = C + S
    M = B * T
    depth = ada_w1.shape[0]
    alpha = (2.0 * depth) ** 0.25
    n_ctx_tiles = C // ROW_TILE
    assert C % TQ == 0 and S % KEY_CHUNK == 0 and ROW_TILE == TQ and B + 1 <= 8
    assert C <= KEY_CHUNK and NA_KEY_ROWS * GRID_W <= KEY_CHUNK

    cos, sin = _rope_tables(C, S)
    xs = jnp.concatenate([ctx, x], axis=1)
    cond = jnp.concatenate([c, c_ctx[None], jnp.zeros((8 - B - 1, D), F32)], axis=0)
    mods = [_modulation(cond, ada_w1, ada_w2, ada_b, i) for i in range(depth)]

    h = _modulate(xs, mods[0], n_ctx_tiles, scale_idx=1, shift_idx=0)
    out = None
    for i in range(depth):
        kind, slot = i % N_MIXERS, i // N_MIXERS
        h2 = h.reshape(M, D)
        if kind == 0:
            lam_init = 0.8 - 0.6 * math.exp(-0.3 * i)
            n_heads = da_w_o.shape[1] // HEAD_V
            qk_cols = 2 * n_heads * HEAD_V
            qkv = _matmul(h2, da_w_qkv, slot, BF16, rope=(cos, sin, T, 0, qk_cols), name="da_qkv")
            lam_params = jnp.zeros((8, LANES), F32).at[:4, :DA_QK_DIM].set(
                jnp.stack([da_lq1[slot], da_lk1[slot], da_lq2[slot], da_lk2[slot]]))
            o = _diff_attention(qkv.reshape(B, T, -1), lam_params, da_sub_g[slot][None], C, lam_init)
            w_o = da_w_o
        elif kind == 1:
            w_down = mla_w_down[slot]
            q_rank = mla_q_g.shape[1]
            kv_rank = mla_kv_g.shape[1]
            n_heads = mla_w_o.shape[1] // HEAD_V
            k_rope_w = w_down[:, q_rank + kv_rank:]
            w_down = jnp.concatenate([w_down, k_rope_w, jnp.zeros((D, LANES), F32)], axis=1)
            down = _matmul(h2, w_down[None], 0, F32, name="mla_down")
            cq, ckv, kr = _mla_norm(down.reshape(B, T, -1), mla_q_g[slot][None], mla_kv_g[slot][None], cos, sin)
            w_uq = mla_w_uq[slot].reshape(q_rank, n_heads, MLA_NOPE + MLA_ROPE)
            w_uq = jnp.concatenate([w_uq[:, :, :MLA_NOPE].reshape(q_rank, -1),
                                    w_uq[:, :, MLA_NOPE:].reshape(q_rank, -1)], axis=1)
            nope_cols = n_heads * MLA_NOPE
            q = _matmul(cq.reshape(M, q_rank), w_uq[None], 0, BF16,
                        rope=(cos, sin, T, nope_cols, nope_cols + n_heads * MLA_ROPE), name="mla_uq")
            w_ukv = mla_w_ukv[slot].reshape(kv_rank, n_heads, MLA_NOPE + HEAD_V)
            w_ukv = jnp.concatenate([w_ukv[:, :, :MLA_NOPE].reshape(kv_rank, -1),
                                     w_ukv[:, :, MLA_NOPE:].reshape(kv_rank, -1)], axis=1)
            kv = _matmul(ckv.reshape(M, kv_rank), w_ukv[None], 0, BF16, name="mla_ukv")
            o = _mla_attention(q.reshape(B, T, -1), kv.reshape(B, T, -1), kr, C)
            w_o = mla_w_o
        else:
            qkv = _matmul(h2, na_w_qkv, slot, BF16, name="na_qkv")
            bias = _na_bias_table(na_rpb[slot], S // GRID_W)
            o = _na_attention(qkv.reshape(B, T, -1), bias, C)
            w_o = na_w_o
        y = _matmul(o.reshape(M, -1), w_o, slot, F32, name="attn_out").reshape(B, T, D)
        xs, h = _ln_mod(xs, y, mods[i], ln_g, ln_b, mods[i], n_ctx_tiles, layer=i, alpha=alpha, gate_idx=2,
                        ln_idx=0, scale_idx=4, shift_idx=3)
        u = _matmul(h.reshape(M, D), mlp_w1, i, BF16, epilogue=_epi_relu2, name="mlp_up")
        y = _matmul(u, mlp_w2, i, F32, name="mlp_down").reshape(B, T, D)
        if i + 1 < depth:
            xs, h = _ln_mod(xs, y, mods[i], ln_g, ln_b, mods[i + 1], n_ctx_tiles, layer=i, alpha=alpha,
                            gate_idx=5, ln_idx=1, scale_idx=1, shift_idx=0)
        else:
            out, _ = _ln_mod(xs, y, mods[i], ln_g, ln_b, None, n_ctx_tiles, layer=i, alpha=alpha, gate_idx=5,
                             ln_idx=1, latent_only=True)
    return out
```

```python
import functools
import math

import numpy as np
import jax
import jax.numpy as jnp
from jax import lax
from jax.experimental import pallas as pl
from jax.experimental.pallas import tpu as pltpu

F32 = jnp.float32
BF16 = jnp.bfloat16

GRID_W = 64
N_MOD = 6
ROPE_BASE = 10000.0
LN_EPS = 1e-5
RMS_EPS = 1e-6
DA_SUB_EPS = 1e-5
DA_QK_DIM = 64
HEAD_V = 128
MLA_NOPE = 128
MLA_ROPE = 64
NA_WIN_ROWS = 8
NA_WIN_COLS = 16
N_MIXERS = 3
MASK_BIAS = -1e30
LOG2E = 1.4426950408889634
TAB_ROT_SCALED, TAB_ROT, TAB_IDENT, TAB_IDENT_SCALED = 0, 1, 2, 3

LANES = 128
MXU_DIM = 256
VMEM_LIMIT_BYTES = 56 * 1024 * 1024

TQ = 256
KEY_CHUNK = 4 * MXU_DIM
FLASH_LOOKAHEAD = 4
DA_HEADS_PER_STEP = 2
MLA_HEADS_PER_STEP = 4
NA_HEADS_PER_STEP = 4
NA_TILE_ROWS = TQ // GRID_W
NA_KEY_ROWS = 12
ROW_TILE = 256
MM_TM = 544
MM_TN = 512
MM_TN_WHOLE_K = 1024
MM_TK_FULL = 4096
MM_TK = 4096
MM_ROW_SPLITS = 4


def _pick_tile(dim, target, align):
    best = None
    for t in range(align, min(dim, target) + 1, align):
        if dim % t == 0:
            best = t
    return best if best is not None else dim


def _params(*sem):
    return pltpu.CompilerParams(dimension_semantics=sem, vmem_limit_bytes=VMEM_LIMIT_BYTES)


def _rope(x, cos, sin):
    n = x.shape[1]
    reps = n // LANES
    if reps > 1:
        cos = jnp.concatenate([cos] * reps, axis=1)
        sin = jnp.concatenate([sin] * reps, axis=1)
    lane = lax.broadcasted_iota(jnp.int32, x.shape, 1)
    nxt = pltpu.roll(x, n - 16, 1)
    prv = pltpu.roll(x, 16, 1)
    partner = jnp.where((lane & 16) == 0, nxt, prv)
    return x * cos + partner * sin


def _mm_body(*refs, nk, epilogue, n_extra, splits):
    x_ref, w_ref = refs[0], refs[1]
    extra = refs[2:2 + n_extra]
    o_ref = refs[2 + n_extra]
    wb_ref = refs[3 + n_extra]
    k = pl.program_id(1)
    i = pl.program_id(2)

    @pl.when(i == 0)
    def _():
        wb_ref[...] = w_ref[...].astype(BF16)

    if nk > 1:
        acc_ref = refs[4 + n_extra]

        @pl.when((pl.program_id(0) == 0) & (k == 0))
        def _():
            acc_ref[i] = jnp.zeros(acc_ref.shape[1:], F32)

    tm = x_ref.shape[0]
    sub = tm // splits
    for r in range(splits):
        rows = slice(r * sub, (r + 1) * sub)
        acc = jnp.dot(x_ref[rows, :], wb_ref[...], preferred_element_type=F32)
        if nk > 1:
            acc = jnp.where(k > 0, acc_ref[i, rows, :], 0.0) + acc
            acc_ref[i, rows, :] = acc
        epilogue(acc, rows, o_ref, *extra)


def _epi_identity(acc, rows, o_ref):
    o_ref[rows, :] = acc.astype(o_ref.dtype)


def _epi_relu2(acc, rows, o_ref):
    r = jnp.maximum(acc, 0.0)
    o_ref[rows, :] = (r * r).astype(o_ref.dtype)


def _epi_rope(acc, rows, o_ref, cos_ref, sin_ref):
    o_ref[rows, :] = _rope(acc, cos_ref[rows, :], sin_ref[rows, :]).astype(o_ref.dtype)


def _matmul(x, w, layer, out_dtype, *, epilogue=_epi_identity, rope=None, name="matmul"):
    M, K = x.shape
    N = w.shape[2]
    tk = K if K <= MM_TK_FULL else _pick_tile(K, MM_TK, LANES)
    nk = K // tk
    tm_target, tn_target = MM_TM, (MM_TN if nk > 1 else MM_TN_WHOLE_K)
    if rope is None:
        tm = _pick_tile(M, tm_target, 16)
        tn = _pick_tile(N, tn_target, LANES)
    else:
        tm = _pick_tile(rope[2], tm_target, 16)
        tn = _pick_tile(math.gcd(N, *[b for lo, hi, _ in rope[3] for b in (lo, hi)]), tn_target, LANES)
    nm = M // tm
    last = nk - 1
    in_specs = [pl.BlockSpec((tm, tk), lambda j, k, i: (i, k)),
                pl.BlockSpec((None, tk, tn), lambda j, k, i: (layer, k, j))]
    operands = [x, w]
    n_extra = 0
    if rope is not None:
        cos, sin, rows_per_batch, ranges = rope
        per = rows_per_batch // tm

        def table_set(j):
            which = TAB_IDENT
            for lo, hi, tab in ranges:
                which = jnp.where((j * tn >= lo) & (j * tn < hi), tab, which)
            return which

        tab_spec = pl.BlockSpec((None, tm, LANES), lambda j, k, i: (table_set(j), i % per, 0))
        in_specs += [tab_spec, tab_spec]
        operands += [cos, sin]
        n_extra = 2
        epilogue = _epi_rope
    scratch = [pltpu.VMEM((tk, tn), BF16)]
    if nk > 1:
        scratch.append(pltpu.VMEM((nm, tm, tn), F32))
    out_spec = pl.BlockSpec((tm, tn), lambda j, k, i: (jnp.where(k == last, i, 0), j))
    return pl.pallas_call(
        functools.partial(_mm_body, nk=nk, epilogue=epilogue, n_extra=n_extra,
                          splits=max(s for s in (MM_ROW_SPLITS, 2, 1) if tm % (16 * s) == 0)),
        grid=(N // tn, nk, nm),
        in_specs=in_specs,
        out_specs=out_spec,
        out_shape=jax.ShapeDtypeStruct((M, N), out_dtype),
        scratch_shapes=scratch,
        compiler_params=_params("parallel", "arbitrary", "arbitrary"),
        name=name,
    )(*operands)


def _mod_body(c_ref, w1_ref, w2_ref, b_ref, o_ref):
    c = c_ref[...]
    s = c * (1.0 / (1.0 + jnp.exp(-c)))
    h = jnp.dot(s.astype(BF16), w1_ref[...].astype(BF16), preferred_element_type=F32)
    o = jnp.dot(h.astype(BF16), w2_ref[...].astype(BF16), preferred_element_type=F32)
    o_ref[...] = o + b_ref[...]


def _modulation(cond, w1, w2, b, layer):
    R, D = cond.shape
    rank = w1.shape[2]
    N = w2.shape[2]
    tn = _pick_tile(N, 2048, LANES)
    out = pl.pallas_call(
        _mod_body,
        grid=(N // tn,),
        in_specs=[pl.BlockSpec((R, D), lambda j: (0, 0)),
                  pl.BlockSpec((None, D, rank), lambda j: (layer, 0, 0)),
                  pl.BlockSpec((None, rank, tn), lambda j: (layer, 0, j)),
                  pl.BlockSpec((None, 1, tn), lambda j: (layer, 0, j))],
        out_specs=pl.BlockSpec((R, tn), lambda j: (0, j)),
        out_shape=jax.ShapeDtypeStruct((R, N), F32),
        compiler_params=_params("arbitrary"),
        name="adaln_modulation",
    )(cond, w1, w2, b.reshape(b.shape[0], 1, N))
    return out.reshape(R, N_MOD, D)


def _modulate_body(x_ref, mod_ref, h_ref, *, scale_idx, shift_idx):
    x = x_ref[0]
    scale = mod_ref[0, scale_idx:scale_idx + 1, :]
    shift = mod_ref[0, shift_idx:shift_idx + 1, :]
    h_ref[0] = (x * (1.0 + scale) + shift).astype(h_ref.dtype)


def _modulate(x, mod, n_ctx_tiles, *, scale_idx, shift_idx):
    B, T, D = x.shape
    ctx_row = B
    return pl.pallas_call(
        functools.partial(_modulate_body, scale_idx=scale_idx, shift_idx=shift_idx),
        grid=(B, T // ROW_TILE),
        in_specs=[pl.BlockSpec((1, ROW_TILE, D), lambda b, t: (b, t, 0)),
                  pl.BlockSpec((1, N_MOD, D), lambda b, t: (jnp.where(t < n_ctx_tiles, ctx_row, b), 0, 0))],
        out_specs=pl.BlockSpec((1, ROW_TILE, D), lambda b, t: (b, t, 0)),
        out_shape=jax.ShapeDtypeStruct((B, T, D), BF16),
        compiler_params=_params("parallel", "parallel"),
        name="modulate",
    )(x, mod)


def _ln_mod_body(x_ref, y_ref, modg_ref, lng_ref, lnb_ref, *rest, alpha, gate_idx, ln_idx, scale_idx,
                 shift_idx, want_h):
    if want_h:
        modn_ref, xo_ref, ho_ref = rest
    else:
        (xo_ref,) = rest
    x = x_ref[0]
    y = y_ref[0]
    gate = modg_ref[0, gate_idx:gate_idx + 1, :]
    z = alpha * x + gate * y
    mu = jnp.mean(z, axis=-1, keepdims=True)
    zc = z - mu
    var = jnp.mean(zc * zc, axis=-1, keepdims=True)
    xn = (zc * lax.rsqrt(var + LN_EPS)) * lng_ref[ln_idx:ln_idx + 1, :] + lnb_ref[ln_idx:ln_idx + 1, :]
    xo_ref[0] = xn
    if want_h:
        scale = modn_ref[0, scale_idx:scale_idx + 1, :]
        shift = modn_ref[0, shift_idx:shift_idx + 1, :]
        ho_ref[0] = (xn * (1.0 + scale) + shift).astype(ho_ref.dtype)


def _ln_mod(x, y, mod_gate, ln_g, ln_b, mod_next, n_ctx_tiles, *, layer, alpha, gate_idx, ln_idx, scale_idx=0,
            shift_idx=0):
    B, T, D = x.shape
    ctx_row = B
    want_h = mod_next is not None
    off = (T - y.shape[1]) // ROW_TILE
    assert off in (0, n_ctx_tiles)
    n_tiles = T // ROW_TILE - off
    x_spec = pl.BlockSpec((1, ROW_TILE, D), lambda b, t: (b, t + off, 0))
    row_out = pl.BlockSpec((1, ROW_TILE, D), lambda b, t: (b, t, 0))
    mod_spec = pl.BlockSpec((1, N_MOD, D), lambda b, t: (jnp.where(t + off < n_ctx_tiles, ctx_row, b), 0, 0))
    ln_spec = pl.BlockSpec((None, 2, D), lambda b, t: (layer, 0, 0))
    in_specs = [x_spec, row_out, mod_spec, ln_spec, ln_spec]
    operands = [x, y, mod_gate, ln_g, ln_b]
    out_specs = [row_out]
    out_shape = [jax.ShapeDtypeStruct((B, n_tiles * ROW_TILE, D), F32)]
    if want_h:
        in_specs.append(mod_spec)
        operands.append(mod_next)
        out_specs.append(row_out)
        out_shape.append(jax.ShapeDtypeStruct((B, n_tiles * ROW_TILE, D), BF16))
    res = pl.pallas_call(
        functools.partial(_ln_mod_body, alpha=alpha, gate_idx=gate_idx, ln_idx=ln_idx, scale_idx=scale_idx,
                          shift_idx=shift_idx, want_h=want_h),
        grid=(B, n_tiles),
        in_specs=in_specs,
        out_specs=out_specs,
        out_shape=out_shape,
        compiler_params=_params("parallel", "parallel"),
        name="resid_ln_modulate",
    )(*operands)
    return (res[0], res[1]) if want_h else (res[0], None)


def _flash(items, s_ref, scale=None, pre_scaled=False):
    c = LOG2E if pre_scaled else (None if scale is None else scale * LOG2E)

    def exp2_scaled(d):
        return jnp.exp2(d if c is None else d * c)

    def put(i):
        it = items[i]
        s = lax.dot_general(it["k"](), it["q"], (((1,), (1,)), ((), ())), preferred_element_type=F32)
        s_ref[i % FLASH_LOOKAHEAD, :it["n"], :] = s

    def pv(it, p):
        return lax.dot_general(it["v"](), p.astype(BF16), (((0,), (0,)), ((), ())), preferred_element_type=F32)

    for i in range(min(FLASH_LOOKAHEAD, len(items))):
        put(i)
    state = None
    for idx, it in enumerate(items):
        s = s_ref[idx % FLASH_LOOKAHEAD, :it["n"], :]
        if idx + FLASH_LOOKAHEAD < len(items):
            put(idx + FLASH_LOOKAHEAD)
        if pre_scaled:
            s = s * scale
            if it.get("bias") is not None:
                s = s + it["bias"]()
        cm = jnp.max(s, axis=0, keepdims=True)
        if it["first"]:
            m = cm
            p = exp2_scaled(s - m)
            l = jnp.sum(p, axis=0, keepdims=True)
            acc = pv(it, p)
        else:
            m_old, l_old, acc_old = state
            m = jnp.maximum(m_old, cm)
            alpha = exp2_scaled(m_old - m)
            p = exp2_scaled(s - m)
            l = alpha * l_old + jnp.sum(p, axis=0, keepdims=True)
            acc = alpha * acc_old + pv(it, p)
        state = (m, l, acc)
        if it["last"]:
            it["emit"]((acc / l).T)


def _score_scratch():
    return pltpu.VMEM((FLASH_LOOKAHEAD, KEY_CHUNK, TQ), F32)


def _key_chunks(n_ctx, n_keys):
    return [(0, n_ctx)] + [(lo, KEY_CHUNK) for lo in range(n_ctx, n_keys, KEY_CHUNK)]


def _problem_items(q, k_get, v_get, chunks, emit):
    return [dict(q=q, k=functools.partial(k_get, lo, n), v=functools.partial(v_get, lo, n), n=n,
                 first=ci == 0, last=ci == len(chunks) - 1, emit=emit)
            for ci, (lo, n) in enumerate(chunks)]


def _for_ctx_and_latent_tiles(qi, n_ctx_tiles, n_ctx, n_keys, run, skip_ctx=False):
    if not skip_ctx:
        @pl.when(qi < n_ctx_tiles)
        def _():
            run(_key_chunks(n_ctx, n_ctx))

    @pl.when(qi >= n_ctx_tiles)
    def _():
        run(_key_chunks(n_ctx, n_keys))


def _da_body(lam_ref, q_ref, k_ref, v_ref, g_ref, o_ref, s_ref, *, heads, n_ctx_tiles, n_ctx, lam_init, q_off):
    lp = lam_ref[...]
    lam = (jnp.exp(jnp.sum(lp[0:1] * lp[1:2], axis=-1, keepdims=True))
           - jnp.exp(jnp.sum(lp[2:3] * lp[3:4], axis=-1, keepdims=True)) + lam_init)
    lane = lax.broadcasted_iota(jnp.int32, (TQ, HEAD_V), 1)

    def run(chunks):
        items = []
        for j in range(heads):
            cols = slice(j * HEAD_V, (j + 1) * HEAD_V)
            q = q_ref[0, :, cols]
            zero = jnp.zeros_like(q)
            comps = [jnp.where(lane < DA_QK_DIM, q, zero), jnp.where(lane >= DA_QK_DIM, q, zero)]
            outs = []

            def finish(o1, cols=cols, outs=outs):
                o = outs[0] - lam * o1
                ms = jnp.mean(o * o, axis=-1, keepdims=True)
                on = (o * lax.rsqrt(ms + DA_SUB_EPS)) * g_ref[...] * (1.0 - lam_init)
                o_ref[0, :, cols] = on.astype(o_ref.dtype)

            def k_get(lo, n, cols=cols):
                return k_ref[0, lo:lo + n, cols]

            def v_get(lo, n, cols=cols):
                return v_ref[0, lo:lo + n, cols]

            items += _problem_items(comps[0], k_get, v_get, chunks, outs.append)
            items += _problem_items(comps[1], k_get, v_get, chunks, finish)
        _flash(items, s_ref)

    _for_ctx_and_latent_tiles(pl.program_id(2) + q_off, n_ctx_tiles, n_ctx, k_ref.shape[1], run,
                              skip_ctx=q_off > 0)


def _diff_attention(qkv, lam_params, sub_g, n_ctx, lam_init, latent_only=False):
    B, T, W3 = qkv.shape
    H = W3 // (3 * HEAD_V)
    G = math.gcd(H, DA_HEADS_PER_STEP)
    W = G * HEAD_V
    q_off = n_ctx // TQ if latent_only else 0
    nq = T // TQ - q_off
    return pl.pallas_call(
        functools.partial(_da_body, heads=G, n_ctx_tiles=n_ctx // TQ, n_ctx=n_ctx, lam_init=lam_init,
                          q_off=q_off),
        grid=(B, H // G, nq),
        in_specs=[pl.BlockSpec((8, LANES), lambda b, g, i: (0, 0)),
                  pl.BlockSpec((1, TQ, W), lambda b, g, i: (b, i + q_off, g)),
                  pl.BlockSpec((1, T, W), lambda b, g, i: (b, 0, H // G + g)),
                  pl.BlockSpec((1, T, W), lambda b, g, i: (b, 0, 2 * (H // G) + g)),
                  pl.BlockSpec((1, HEAD_V), lambda b, g, i: (0, 0))],
        out_specs=pl.BlockSpec((1, TQ, W), lambda b, g, i: (b, i, g)),
        out_shape=jax.ShapeDtypeStruct((B, nq * TQ, H * HEAD_V), BF16),
        scratch_shapes=[_score_scratch()],
        compiler_params=_params("parallel", "parallel", "arbitrary"),
        name="diff_attention",
    )(lam_params, qkv, qkv, qkv, sub_g)


def _mla_norm_body(d_ref, qg_ref, kvg_ref, cos_ref, sin_ref, cq_ref, ckv_ref, kr_ref, *, q_rank, kv_rank):
    d = d_ref[0]

    def rms(x, g):
        ms = jnp.mean(x * x, axis=-1, keepdims=True)
        return (x * lax.rsqrt(ms + RMS_EPS)) * g

    cq_ref[0] = rms(d[:, :q_rank], qg_ref[...]).astype(cq_ref.dtype)
    ckv_ref[0] = rms(d[:, q_rank:q_rank + kv_rank], kvg_ref[...]).astype(ckv_ref.dtype)
    kr = d[:, q_rank + kv_rank:q_rank + kv_rank + LANES]
    kr_ref[0] = _rope(kr, cos_ref[...], sin_ref[...]).astype(kr_ref.dtype)


def _mla_norm(down, q_g, kv_g, cos, sin):
    B, T, W = down.shape
    q_rank = q_g.shape[1]
    kv_rank = kv_g.shape[1]
    row = lambda w: pl.BlockSpec((1, ROW_TILE, w), lambda b, t: (b, t, 0))
    full = lambda w: pl.BlockSpec((1, w), lambda b, t: (0, 0))
    tab = pl.BlockSpec((None, ROW_TILE, LANES), lambda b, t: (TAB_ROT, t, 0))
    return pl.pallas_call(
        functools.partial(_mla_norm_body, q_rank=q_rank, kv_rank=kv_rank),
        grid=(B, T // ROW_TILE),
        in_specs=[row(W), full(q_rank), full(kv_rank), tab, tab],
        out_specs=[row(q_rank), row(kv_rank), row(LANES)],
        out_shape=[jax.ShapeDtypeStruct((B, T, q_rank), BF16),
                   jax.ShapeDtypeStruct((B, T, kv_rank), BF16),
                   jax.ShapeDtypeStruct((B, T, LANES), BF16)],
        compiler_params=_params("parallel", "parallel"),
        name="mla_norm_rope",
    )(down, q_g, kv_g, cos, sin)


def _mla_body(qn_ref, qr_ref, kn_ref, kr_ref, v_ref, o_ref, s_ref, *, heads, n_ctx_tiles, n_ctx):
    lane = lax.broadcasted_iota(jnp.int32, (TQ, LANES), 1)

    def run(chunks):
        items = []
        for j in range(heads):
            cols = slice(j * HEAD_V, (j + 1) * HEAD_V)
            pair = slice((j // 2) * LANES, (j // 2 + 1) * LANES)
            qr = qr_ref[0, :, pair]
            mine = (lane // MLA_ROPE) == (j % 2)
            q = jnp.concatenate([qn_ref[0, :, cols], jnp.where(mine, qr, jnp.zeros_like(qr))], axis=1)

            def k_get(lo, n, cols=cols):
                return jnp.concatenate([kn_ref[0, lo:lo + n, cols], kr_ref[0, lo:lo + n, :]], axis=1)

            def v_get(lo, n, cols=cols):
                return v_ref[0, lo:lo + n, cols]

            def emit(o, cols=cols):
                o_ref[0, :, cols] = o.astype(o_ref.dtype)

            items += _problem_items(q, k_get, v_get, chunks, emit)
        _flash(items, s_ref)

    _for_ctx_and_latent_tiles(pl.program_id(2), n_ctx_tiles, n_ctx, kn_ref.shape[1], run)


def _mla_attention(q, kv, kr, n_ctx):
    B, T, Wq = q.shape
    H = Wq // (MLA_NOPE + MLA_ROPE)
    nq = T // TQ
    G = math.gcd(H, MLA_HEADS_PER_STEP)
    assert G % 2 == 0
    W = G * HEAD_V
    Wr = G * MLA_ROPE
    return pl.pallas_call(
        functools.partial(_mla_body, heads=G, n_ctx_tiles=n_ctx // TQ, n_ctx=n_ctx),
        grid=(B, H // G, nq),
        in_specs=[pl.BlockSpec((1, TQ, W), lambda b, g, i: (b, i, g)),
                  pl.BlockSpec((1, TQ, Wr), lambda b, g, i: (b, i, (H * MLA_NOPE) // Wr + g)),
                  pl.BlockSpec((1, T, W), lambda b, g, i: (b, 0, g)),
                  pl.BlockSpec((1, T, LANES), lambda b, g, i: (b, 0, 0)),
                  pl.BlockSpec((1, T, W), lambda b, g, i: (b, 0, H // G + g))],
        out_specs=pl.BlockSpec((1, TQ, W), lambda b, g, i: (b, i, g)),
        out_shape=jax.ShapeDtypeStruct((B, T, H * HEAD_V), BF16),
        scratch_shapes=[_score_scratch()],
        compiler_params=_params("parallel", "parallel", "arbitrary"),
        name="mla_attention",
    )(q, q, kv, kr, kv)


def _na_key_row_start(i, n_rows):
    return jnp.clip(NA_TILE_ROWS * i - NA_WIN_ROWS // 2, 0, n_rows - NA_KEY_ROWS)


def _na_body(q_ref, k_ref, v_ref, bias_ref, o_ref, s_ref, *, heads, n_ctx_tiles, n_ctx, n_rows):
    qi = pl.program_id(2)
    n_lat = NA_KEY_ROWS * GRID_W

    def run(latent):
        start = pl.multiple_of(n_ctx + _na_key_row_start(qi - n_ctx_tiles, n_rows) * GRID_W, GRID_W)
        items = []
        for j in range(heads):
            cols = slice(j * HEAD_V, (j + 1) * HEAD_V)

            def emit(o, cols=cols):
                o_ref[0, :, cols] = o.astype(o_ref.dtype)

            ctx_item = dict(q=q_ref[0, :, cols], n=n_ctx, first=True, last=not latent, emit=emit,
                            k=lambda cols=cols: k_ref[0, :n_ctx, cols],
                            v=lambda cols=cols: v_ref[0, :n_ctx, cols])
            items.append(ctx_item)
            if latent:
                items.append(dict(q=q_ref[0, :, cols], n=n_lat, first=False, last=True, emit=emit,
                                  k=lambda cols=cols: k_ref[0, pl.ds(start, n_lat), cols],
                                  v=lambda cols=cols: v_ref[0, pl.ds(start, n_lat), cols],
                                  bias=lambda j=j: bias_ref[j]))
        _flash(items, s_ref, HEAD_V ** -0.5, pre_scaled=True)

    @pl.when(qi < n_ctx_tiles)
    def _():
        run(False)

    @pl.when(qi >= n_ctx_tiles)
    def _():
        run(True)


def _na_bias_table(rpb, n_rows):
    H = rpb.shape[0]
    n_dr = 2 * NA_WIN_ROWS - 1
    W = GRID_W
    off_idx = np.clip(np.arange(2 * W - 1) - (W - 1) + NA_WIN_COLS - 1, 0, 2 * NA_WIN_COLS - 2)
    ext = rpb[:, :, off_idx]
    ext = jnp.broadcast_to(ext[:, :, None, :], (H, n_dr, W, 2 * W - 1))
    ext = jnp.pad(ext, ((0, 0), (0, 0), (0, 0), (0, 1))).reshape(H, n_dr, W * 2 * W)
    skew = ext[:, :, W - 1:W - 1 + W * (2 * W - 1)].reshape(H, n_dr, W, 2 * W - 1)[..., :W]
    col = np.arange(W)
    col_start = np.clip(col - NA_WIN_COLS // 2, 0, W - NA_WIN_COLS)
    col_in = (col[None, :] >= col_start[:, None]) & (col[None, :] < col_start[:, None] + NA_WIN_COLS)
    col_bias = jnp.where(col_in[None, None], skew, MASK_BIAS)
    col_bias = jnp.concatenate([col_bias, jnp.full((H, 1, W, W), MASK_BIAS, F32)], axis=1)
    n_tiles = n_rows // NA_TILE_ROWS
    tiles = np.array([0, 1, n_tiles - 1])
    ks = np.clip(NA_TILE_ROWS * tiles - NA_WIN_ROWS // 2, 0, n_rows - NA_KEY_ROWS)
    qrow = NA_TILE_ROWS * tiles[:, None] + np.arange(NA_TILE_ROWS)[None]
    krow = ks[:, None] + np.arange(NA_KEY_ROWS)[None]
    r0 = np.clip(qrow - NA_WIN_ROWS // 2, 0, n_rows - NA_WIN_ROWS)
    row_in = (krow[:, None, :] >= r0[:, :, None]) & (krow[:, None, :] < r0[:, :, None] + NA_WIN_ROWS)
    row_idx = np.where(row_in, krow[:, None, :] - qrow[:, :, None] + NA_WIN_ROWS - 1, n_dr)
    col_bias_t = jnp.swapaxes(col_bias, 2, 3)
    blocks = [jnp.concatenate([col_bias_t[:, int(row_idx[v, j, r])] for j in range(NA_TILE_ROWS)], axis=-1)
              for v in range(3) for r in range(NA_KEY_ROWS)]
    return jnp.stack(blocks, axis=1).reshape(H, 3, NA_KEY_ROWS * W, TQ)


def _na_attention(qkv, bias, n_ctx):
    B, T, W3 = qkv.shape
    H = W3 // (3 * HEAD_V)
    G = math.gcd(H, NA_HEADS_PER_STEP)
    W = G * HEAD_V
    nq = T // TQ
    n_ctx_tiles = n_ctx // TQ
    n_rows = (T - n_ctx) // GRID_W
    n_lat_tiles = nq - n_ctx_tiles

    def bias_index(b, g, i):
        t = i - n_ctx_tiles
        return (g, jnp.where(t <= 0, 0, jnp.where(t == n_lat_tiles - 1, 2, 1)), 0, 0)

    return pl.pallas_call(
        functools.partial(_na_body, heads=G, n_ctx_tiles=n_ctx_tiles, n_ctx=n_ctx, n_rows=n_rows),
        grid=(B, H // G, nq),
        in_specs=[pl.BlockSpec((1, TQ, W), lambda b, g, i: (b, i, g)),
                  pl.BlockSpec((1, T, W), lambda b, g, i: (b, 0, H // G + g)),
                  pl.BlockSpec((1, T, W), lambda b, g, i: (b, 0, 2 * (H // G) + g)),
                  pl.BlockSpec((G, None, NA_KEY_ROWS * GRID_W, TQ), bias_index)],
        out_specs=pl.BlockSpec((1, TQ, W), lambda b, g, i: (b, i, g)),
        out_shape=jax.ShapeDtypeStruct((B, T, H * HEAD_V), BF16),
        scratch_shapes=[_score_scratch()],
        compiler_params=_params("parallel", "parallel", "arbitrary"),
        name="neighbourhood_attention",
    )(qkv, qkv, qkv, bias)


def _rope_tables(n_ctx, seq, q_scale):
    pos = jnp.arange(seq)
    quarter = DA_QK_DIM // 4
    inv = ROPE_BASE ** (-jnp.arange(quarter, dtype=F32) / quarter)
    ang_r = (pos // GRID_W).astype(F32)[:, None] * inv
    ang_c = (pos % GRID_W).astype(F32)[:, None] * inv
    cos = jnp.concatenate([jnp.cos(ang_r)] * 2 + [jnp.cos(ang_c)] * 2, axis=1)
    sin = jnp.concatenate([-jnp.sin(ang_r), jnp.sin(ang_r), -jnp.sin(ang_c), jnp.sin(ang_c)], axis=1)
    cos = jnp.concatenate([jnp.ones((n_ctx, DA_QK_DIM), F32), cos], axis=0)
    sin = jnp.concatenate([jnp.zeros((n_ctx, DA_QK_DIM), F32), sin], axis=0)
    cos = jnp.concatenate([cos, cos], axis=1)
    sin = jnp.concatenate([sin, sin], axis=1)
    one, zero = jnp.ones_like(cos), jnp.zeros_like(sin)
    return (jnp.stack([cos * q_scale, cos, one, one * q_scale]),
            jnp.stack([sin * q_scale, sin, zero, zero]))


def kernel(x, c, ctx, c_ctx, ada_w1, ada_w2, ada_b, ln_g, ln_b, mlp_w1, mlp_w2, da_w_qkv, da_w_o, da_lq1,
           da_lk1, da_lq2, da_lk2, da_sub_g, mla_w_down, mla_q_g, mla_w_uq, mla_kv_g, mla_w_ukv, mla_w_o,
           na_w_qkv, na_w_o, na_rpb):
    B, S, D = x.shape
    C = ctx.shape[1]
    T = C + S
    M = B * T
    depth = ada_w1.shape[0]
    alpha = (2.0 * depth) ** 0.25
    n_ctx_tiles = C // ROW_TILE
    assert C % TQ == 0 and S % KEY_CHUNK == 0 and ROW_TILE == TQ and B + 1 <= 8
    assert C <= KEY_CHUNK and NA_KEY_ROWS * GRID_W <= KEY_CHUNK

    da_tabs = _rope_tables(C, S, (DA_QK_DIM ** -0.5) * LOG2E)
    mla_tabs = _rope_tables(C, S, ((MLA_NOPE + MLA_ROPE) ** -0.5) * LOG2E)
    xs = jnp.concatenate([ctx, x], axis=1)
    cond = jnp.concatenate([c, c_ctx[None], jnp.zeros((8 - B - 1, D), F32)], axis=0)
    mods = [_modulation(cond, ada_w1, ada_w2, ada_b, i) for i in range(depth)]

    h = _modulate(xs, mods[0], n_ctx_tiles, scale_idx=1, shift_idx=0)
    for i in range(depth):
        kind, slot = i % N_MIXERS, i // N_MIXERS
        last = i == depth - 1
        h2 = h.reshape(M, D)
        if kind == 0:
            lam_init = 0.8 - 0.6 * math.exp(-0.3 * i)
            n_heads = da_w_o.shape[1] // HEAD_V
            q_cols = n_heads * HEAD_V
            qkv = _matmul(h2, da_w_qkv, slot, BF16, name="da_qkv",
                          rope=(*da_tabs, T, [(0, q_cols, TAB_ROT_SCALED), (q_cols, 2 * q_cols, TAB_ROT)]))
            lam_params = jnp.zeros((8, LANES), F32).at[:4, :DA_QK_DIM].set(
                jnp.stack([da_lq1[slot], da_lk1[slot], da_lq2[slot], da_lk2[slot]]))
            o = _diff_attention(qkv.reshape(B, T, -1), lam_params, da_sub_g[slot][None], C, lam_init,
                                latent_only=last)
            w_o = da_w_o
        elif kind == 1:
            w_down = mla_w_down[slot]
            q_rank = mla_q_g.shape[1]
            kv_rank = mla_kv_g.shape[1]
            n_heads = mla_w_o.shape[1] // HEAD_V
            k_rope_w = w_down[:, q_rank + kv_rank:]
            w_down = jnp.concatenate([w_down, k_rope_w, jnp.zeros((D, LANES), F32)], axis=1)
            down = _matmul(h2, w_down[None], 0, F32, name="mla_down")
            cq, ckv, kr = _mla_norm(down.reshape(B, T, -1), mla_q_g[slot][None], mla_kv_g[slot][None], *mla_tabs)
            w_uq = mla_w_uq[slot].reshape(q_rank, n_heads, MLA_NOPE + MLA_ROPE)
            w_uq = jnp.concatenate([w_uq[:, :, :MLA_NOPE].reshape(q_rank, -1),
                                    w_uq[:, :, MLA_NOPE:].reshape(q_rank, -1)], axis=1)
            nope_cols = n_heads * MLA_NOPE
            q = _matmul(cq.reshape(M, q_rank), w_uq[None], 0, BF16,
                        rope=(*mla_tabs, T, [(0, nope_cols, TAB_IDENT_SCALED),
                                             (nope_cols, nope_cols + n_heads * MLA_ROPE, TAB_ROT_SCALED)]),
                        name="mla_uq")
            w_ukv = mla_w_ukv[slot].reshape(kv_rank, n_heads, MLA_NOPE + HEAD_V)
            w_ukv = jnp.concatenate([w_ukv[:, :, :MLA_NOPE].reshape(kv_rank, -1),
                                     w_ukv[:, :, MLA_NOPE:].reshape(kv_rank, -1)], axis=1)
            kv = _matmul(ckv.reshape(M, kv_rank), w_ukv[None], 0, BF16, name="mla_ukv")
            o = _mla_attention(q.reshape(B, T, -1), kv.reshape(B, T, -1), kr, C)
            w_o = mla_w_o
        else:
            qkv = _matmul(h2, na_w_qkv, slot, BF16, name="na_qkv")
            bias = _na_bias_table(na_rpb[slot], S // GRID_W)
            o = _na_attention(qkv.reshape(B, T, -1), bias, C)
            w_o = na_w_o
        if last and o.shape[1] == T:
            o = o[:, C:]
        rows = o.shape[1]
        y = _matmul(o.reshape(B * rows, -1), w_o, slot, F32, name="attn_out").reshape(B, rows, D)
        xs, h = _ln_mod(xs, y, mods[i], ln_g, ln_b, mods[i], n_ctx_tiles, layer=i, alpha=alpha, gate_idx=2,
                        ln_idx=0, scale_idx=4, shift_idx=3)
        u = _matmul(h.reshape(B * rows, D), mlp_w1, i, BF16, epilogue=_epi_relu2, name="mlp_up")
        y = _matmul(u, mlp_w2, i, F32, name="mlp_down").reshape(B, rows, D)
        if not last:
            xs, h = _ln_mod(xs, y, mods[i], ln_g, ln_b, mods[i + 1], n_ctx_tiles, layer=i, alpha=alpha,
                            gate_idx=5, ln_idx=1, scale_idx=1, shift_idx=0)
        else:
            xs, _ = _ln_mod(xs, y, mods[i], ln_g, ln_b, None, 0, layer=i, alpha=alpha, gate_idx=5, ln_idx=1)
    return xs
```

```python
import functools
import math

import numpy as np
import jax
import jax.numpy as jnp
from jax import lax
from jax.experimental import pallas as pl
from jax.experimental.pallas import tpu as pltpu

F32 = jnp.float32
BF16 = jnp.bfloat16

GRID_W = 64
N_MOD = 6
ROPE_BASE = 10000.0
LN_EPS = 1e-5
RMS_EPS = 1e-6
DA_SUB_EPS = 1e-5
DA_QK_DIM = 64
HEAD_V = 128
MLA_NOPE = 128
MLA_ROPE = 64
NA_WIN_ROWS = 8
NA_WIN_COLS = 16
N_MIXERS = 3
MASK_BIAS = -1e30
LOG2E = 1.4426950408889634
TAB_ROT_SCALED, TAB_ROT, TAB_IDENT, TAB_IDENT_SCALED = 0, 1, 2, 3

LANES = 128
MXU_DIM = 256
VMEM_LIMIT_BYTES = 56 * 1024 * 1024

TQ = 256
KEY_CHUNK = 4 * MXU_DIM
FLASH_LOOKAHEAD = 4
DA_HEADS_PER_STEP = 2
MLA_HEADS_PER_STEP = 4
NA_HEADS_PER_STEP = 4
NA_TILE_ROWS = TQ // GRID_W
NA_KEY_ROWS = 12
ROW_TILE = 256
MM_TM = 544
MM_TN = 512
MM_TN_WHOLE_K = 1024
MM_TK_FULL = 4096
MM_TK = 4096
MM_ROW_SPLITS = 4


def _pick_tile(dim, target, align):
    best = None
    for t in range(align, min(dim, target) + 1, align):
        if dim % t == 0:
            best = t
    return best if best is not None else dim


def _params(*sem):
    return pltpu.CompilerParams(dimension_semantics=sem, vmem_limit_bytes=VMEM_LIMIT_BYTES)


def _rope(x, cos, sin):
    n = x.shape[1]
    reps = n // LANES
    if reps > 1:
        cos = jnp.concatenate([cos] * reps, axis=1)
        sin = jnp.concatenate([sin] * reps, axis=1)
    lane = lax.broadcasted_iota(jnp.int32, x.shape, 1)
    nxt = pltpu.roll(x, n - 16, 1)
    prv = pltpu.roll(x, 16, 1)
    partner = jnp.where((lane & 16) == 0, nxt, prv)
    return x * cos + partner * sin


def _mm_body(*refs, nk, epilogue, n_extra, splits):
    x_ref, w_ref = refs[0], refs[1]
    extra = refs[2:2 + n_extra]
    o_ref = refs[2 + n_extra]
    wb_ref = refs[3 + n_extra]
    k = pl.program_id(1)
    i = pl.program_id(2)

    @pl.when(i == 0)
    def _():
        wb_ref[...] = w_ref[...].astype(BF16)

    if nk > 1:
        acc_ref = refs[4 + n_extra]

        @pl.when((pl.program_id(0) == 0) & (k == 0))
        def _():
            acc_ref[i] = jnp.zeros(acc_ref.shape[1:], F32)

    tm = x_ref.shape[0]
    sub = tm // splits
    for r in range(splits):
        rows = slice(r * sub, (r + 1) * sub)
        acc = jnp.dot(x_ref[rows, :], wb_ref[...], preferred_element_type=F32)
        if nk > 1:
            acc = jnp.where(k > 0, acc_ref[i, rows, :], 0.0) + acc
            acc_ref[i, rows, :] = acc
        epilogue(acc, rows, o_ref, *extra)


def _epi_identity(acc, rows, o_ref):
    o_ref[rows, :] = acc.astype(o_ref.dtype)


def _epi_relu2(acc, rows, o_ref):
    r = jnp.maximum(acc, 0.0)
    o_ref[rows, :] = (r * r).astype(o_ref.dtype)


def _epi_rope(acc, rows, o_ref, cos_ref, sin_ref):
    o_ref[rows, :] = _rope(acc, cos_ref[rows, :], sin_ref[rows, :]).astype(o_ref.dtype)


def _matmul(x, w, layer, out_dtype, *, epilogue=_epi_identity, rope=None, name="matmul"):
    M, K = x.shape
    N = w.shape[2]
    tk = K if K <= MM_TK_FULL else _pick_tile(K, MM_TK, LANES)
    nk = K // tk
    tm_target, tn_target = MM_TM, (MM_TN if nk > 1 else MM_TN_WHOLE_K)
    if rope is None:
        tm = _pick_tile(M, tm_target, 16)
        tn = _pick_tile(N, tn_target, LANES)
    else:
        tm = _pick_tile(rope[2], tm_target, 16)
        tn = _pick_tile(math.gcd(N, *[b for lo, hi, _ in rope[3] for b in (lo, hi)]), tn_target, LANES)
    nm = M // tm
    last = nk - 1
    in_specs = [pl.BlockSpec((tm, tk), lambda j, k, i: (i, k)),
                pl.BlockSpec((None, tk, tn), lambda j, k, i: (layer, k, j))]
    operands = [x, w]
    n_extra = 0
    if rope is not None:
        cos, sin, rows_per_batch, ranges = rope
        per = rows_per_batch // tm

        def table_set(j):
            which = TAB_IDENT
            for lo, hi, tab in ranges:
                which = jnp.where((j * tn >= lo) & (j * tn < hi), tab, which)
            return which

        tab_spec = pl.BlockSpec((None, tm, LANES), lambda j, k, i: (table_set(j), i % per, 0))
        in_specs += [tab_spec, tab_spec]
        operands += [cos, sin]
        n_extra = 2
        epilogue = _epi_rope
    scratch = [pltpu.VMEM((tk, tn), BF16)]
    if nk > 1:
        scratch.append(pltpu.VMEM((nm, tm, tn), F32))
    out_spec = pl.BlockSpec((tm, tn), lambda j, k, i: (jnp.where(k == last, i, 0), j))
    return pl.pallas_call(
        functools.partial(_mm_body, nk=nk, epilogue=epilogue, n_extra=n_extra,
                          splits=max(s for s in (MM_ROW_SPLITS, 2, 1) if tm % (16 * s) == 0)),
        grid=(N // tn, nk, nm),
        in_specs=in_specs,
        out_specs=out_spec,
        out_shape=jax.ShapeDtypeStruct((M, N), out_dtype),
        scratch_shapes=scratch,
        compiler_params=_params("parallel", "arbitrary", "arbitrary"),
        name=name,
    )(*operands)


def _mod_body(c_ref, w1_ref, w2_ref, b_ref, o_ref):
    c = c_ref[...]
    s = c * (1.0 / (1.0 + jnp.exp(-c)))
    h = jnp.dot(s.astype(BF16), w1_ref[...].astype(BF16), preferred_element_type=F32)
    o = jnp.dot(h.astype(BF16), w2_ref[...].astype(BF16), preferred_element_type=F32)
    o_ref[...] = o + b_ref[...]


def _modulation(cond, w1, w2, b, layer):
    R, D = cond.shape
    rank = w1.shape[2]
    N = w2.shape[2]
    tn = _pick_tile(N, 2048, LANES)
    out = pl.pallas_call(
        _mod_body,
        grid=(N // tn,),
        in_specs=[pl.BlockSpec((R, D), lambda j: (0, 0)),
                  pl.BlockSpec((None, D, rank), lambda j: (layer, 0, 0)),
                  pl.BlockSpec((None, rank, tn), lambda j: (layer, 0, j)),
                  pl.BlockSpec((None, 1, tn), lambda j: (layer, 0, j))],
        out_specs=pl.BlockSpec((R, tn), lambda j: (0, j)),
        out_shape=jax.ShapeDtypeStruct((R, N), F32),
        compiler_params=_params("arbitrary"),
        name="adaln_modulation",
    )(cond, w1, w2, b.reshape(b.shape[0], 1, N))
    return out.reshape(R, N_MOD, D)


def _modulate_body(x_ref, mod_ref, h_ref, *, scale_idx, shift_idx):
    x = x_ref[0]
    scale = mod_ref[0, scale_idx:scale_idx + 1, :]
    shift = mod_ref[0, shift_idx:shift_idx + 1, :]
    h_ref[0] = (x * (1.0 + scale) + shift).astype(h_ref.dtype)


def _modulate(x, mod, n_ctx_tiles, *, scale_idx, shift_idx):
    B, T, D = x.shape
    ctx_row = B
    return pl.pallas_call(
        functools.partial(_modulate_body, scale_idx=scale_idx, shift_idx=shift_idx),
        grid=(B, T // ROW_TILE),
        in_specs=[pl.BlockSpec((1, ROW_TILE, D), lambda b, t: (b, t, 0)),
                  pl.BlockSpec((1, N_MOD, D), lambda b, t: (jnp.where(t < n_ctx_tiles, ctx_row, b), 0, 0))],
        out_specs=pl.BlockSpec((1, ROW_TILE, D), lambda b, t: (b, t, 0)),
        out_shape=jax.ShapeDtypeStruct((B, T, D), BF16),
        compiler_params=_params("parallel", "parallel"),
        name="modulate",
    )(x, mod)


def _ln_mod_body(x_ref, y_ref, modg_ref, lng_ref, lnb_ref, *rest, alpha, gate_idx, ln_idx, scale_idx,
                 shift_idx, want_h):
    if want_h:
        modn_ref, xo_ref, ho_ref = rest
    else:
        (xo_ref,) = rest
    x = x_ref[0]
    y = y_ref[0]
    gate = modg_ref[0, gate_idx:gate_idx + 1, :]
    z = alpha * x + gate * y
    mu = jnp.mean(z, axis=-1, keepdims=True)
    zc = z - mu
    var = jnp.mean(zc * zc, axis=-1, keepdims=True)
    xn = (zc * lax.rsqrt(var + LN_EPS)) * lng_ref[ln_idx:ln_idx + 1, :] + lnb_ref[ln_idx:ln_idx + 1, :]
    xo_ref[0] = xn
    if want_h:
        scale = modn_ref[0, scale_idx:scale_idx + 1, :]
        shift = modn_ref[0, shift_idx:shift_idx + 1, :]
        ho_ref[0] = (xn * (1.0 + scale) + shift).astype(ho_ref.dtype)


def _ln_mod(x, y, mod_gate, ln_g, ln_b, mod_next, n_ctx_tiles, *, layer, alpha, gate_idx, ln_idx, scale_idx=0,
            shift_idx=0):
    B, T, D = x.shape
    ctx_row = B
    want_h = mod_next is not None
    off = (T - y.shape[1]) // ROW_TILE
    assert off in (0, n_ctx_tiles)
    n_tiles = T // ROW_TILE - off
    x_spec = pl.BlockSpec((1, ROW_TILE, D), lambda b, t: (b, t + off, 0))
    row_out = pl.BlockSpec((1, ROW_TILE, D), lambda b, t: (b, t, 0))
    mod_spec = pl.BlockSpec((1, N_MOD, D), lambda b, t: (jnp.where(t + off < n_ctx_tiles, ctx_row, b), 0, 0))
    ln_spec = pl.BlockSpec((None, 2, D), lambda b, t: (layer, 0, 0))
    in_specs = [x_spec, row_out, mod_spec, ln_spec, ln_spec]
    operands = [x, y, mod_gate, ln_g, ln_b]
    out_specs = [row_out]
    out_shape = [jax.ShapeDtypeStruct((B, n_tiles * ROW_TILE, D), F32)]
    if want_h:
        in_specs.append(mod_spec)
        operands.append(mod_next)
        out_specs.append(row_out)
        out_shape.append(jax.ShapeDtypeStruct((B, n_tiles * ROW_TILE, D), BF16))
    res = pl.pallas_call(
        functools.partial(_ln_mod_body, alpha=alpha, gate_idx=gate_idx, ln_idx=ln_idx, scale_idx=scale_idx,
                          shift_idx=shift_idx, want_h=want_h),
        grid=(B, n_tiles),
        in_specs=in_specs,
        out_specs=out_specs,
        out_shape=out_shape,
        compiler_params=_params("parallel", "parallel"),
        name="resid_ln_modulate",
    )(*operands)
    return (res[0], res[1]) if want_h else (res[0], None)


def _flash(items, s_ref, scale=None, pre_scaled=False):
    c = LOG2E if pre_scaled else (None if scale is None else scale * LOG2E)

    def exp2_scaled(d):
        return jnp.exp2(d if c is None else d * c)

    def put(i):
        it = items[i]
        s = lax.dot_general(it["k"](), it["q"], (((1,), (1,)), ((), ())), preferred_element_type=F32)
        s_ref[i % FLASH_LOOKAHEAD, :it["n"], :] = s

    def pv(it, p):
        return lax.dot_general(it["v"](), p.astype(BF16), (((0,), (0,)), ((), ())), preferred_element_type=F32)

    for i in range(min(FLASH_LOOKAHEAD, len(items))):
        put(i)
    state = None
    for idx, it in enumerate(items):
        s = s_ref[idx % FLASH_LOOKAHEAD, :it["n"], :]
        if idx + FLASH_LOOKAHEAD < len(items):
            put(idx + FLASH_LOOKAHEAD)
        if pre_scaled:
            s = s * scale
            if it.get("bias") is not None:
                s = s + it["bias"]()
        cm = jnp.max(s, axis=0, keepdims=True)
        if it["first"]:
            m = cm
            p = exp2_scaled(s - m)
            l = jnp.sum(p, axis=0, keepdims=True)
            acc = pv(it, p)
        else:
            m_old, l_old, acc_old = state
            m = jnp.maximum(m_old, cm)
            alpha = exp2_scaled(m_old - m)
            p = exp2_scaled(s - m)
            l = alpha * l_old + jnp.sum(p, axis=0, keepdims=True)
            acc = alpha * acc_old + pv(it, p)
        state = (m, l, acc)
        if it["last"]:
            it["emit"]((acc / l).T)


def _score_scratch():
    return pltpu.VMEM((FLASH_LOOKAHEAD, KEY_CHUNK, TQ), F32)


def _key_chunks(n_ctx, n_keys):
    return [(0, n_ctx)] + [(lo, KEY_CHUNK) for lo in range(n_ctx, n_keys, KEY_CHUNK)]


def _problem_items(q, k_get, v_get, chunks, emit):
    return [dict(q=q, k=functools.partial(k_get, lo, n), v=functools.partial(v_get, lo, n), n=n,
                 first=ci == 0, last=ci == len(chunks) - 1, emit=emit)
            for ci, (lo, n) in enumerate(chunks)]


def _for_ctx_and_latent_tiles(qi, n_ctx_tiles, n_ctx, n_keys, run):
    @pl.when(qi < n_ctx_tiles)
    def _():
        run(_key_chunks(n_ctx, n_ctx))

    @pl.when(qi >= n_ctx_tiles)
    def _():
        run(_key_chunks(n_ctx, n_keys))


def _da_body(lam_ref, q_ref, k_ref, v_ref, g_ref, o_ref, s_ref, *, heads, n_ctx_tiles, n_ctx, lam_init):
    lp = lam_ref[...]
    lam = (jnp.exp(jnp.sum(lp[0:1] * lp[1:2], axis=-1, keepdims=True))
           - jnp.exp(jnp.sum(lp[2:3] * lp[3:4], axis=-1, keepdims=True)) + lam_init)
    lane = lax.broadcasted_iota(jnp.int32, (TQ, HEAD_V), 1)

    def run(chunks):
        items = []
        for j in range(heads):
            cols = slice(j * HEAD_V, (j + 1) * HEAD_V)
            q = q_ref[0, :, cols]
            zero = jnp.zeros_like(q)
            comps = [jnp.where(lane < DA_QK_DIM, q, zero), jnp.where(lane >= DA_QK_DIM, q, zero)]
            outs = []

            def finish(o1, cols=cols, outs=outs):
                o = outs[0] - lam * o1
                ms = jnp.mean(o * o, axis=-1, keepdims=True)
                on = (o * lax.rsqrt(ms + DA_SUB_EPS)) * g_ref[...] * (1.0 - lam_init)
                o_ref[0, :, cols] = on.astype(o_ref.dtype)

            def k_get(lo, n, cols=cols):
                return k_ref[0, lo:lo + n, cols]

            def v_get(lo, n, cols=cols):
                return v_ref[0, lo:lo + n, cols]

            items += _problem_items(comps[0], k_get, v_get, chunks, outs.append)
            items += _problem_items(comps[1], k_get, v_get, chunks, finish)
        _flash(items, s_ref)

    _for_ctx_and_latent_tiles(pl.program_id(2), n_ctx_tiles, n_ctx, k_ref.shape[1], run)


def _diff_attention(qkv, lam_params, sub_g, n_ctx, lam_init):
    B, T, W3 = qkv.shape
    H = W3 // (3 * HEAD_V)
    G = math.gcd(H, DA_HEADS_PER_STEP)
    W = G * HEAD_V
    nq = T // TQ
    return pl.pallas_call(
        functools.partial(_da_body, heads=G, n_ctx_tiles=n_ctx // TQ, n_ctx=n_ctx, lam_init=lam_init),
        grid=(B, H // G, nq),
        in_specs=[pl.BlockSpec((8, LANES), lambda b, g, i: (0, 0)),
                  pl.BlockSpec((1, TQ, W), lambda b, g, i: (b, i, g)),
                  pl.BlockSpec((1, T, W), lambda b, g, i: (b, 0, H // G + g)),
                  pl.BlockSpec((1, T, W), lambda b, g, i: (b, 0, 2 * (H // G) + g)),
                  pl.BlockSpec((1, HEAD_V), lambda b, g, i: (0, 0))],
        out_specs=pl.BlockSpec((1, TQ, W), lambda b, g, i: (b, i, g)),
        out_shape=jax.ShapeDtypeStruct((B, nq * TQ, H * HEAD_V), BF16),
        scratch_shapes=[_score_scratch()],
        compiler_params=_params("parallel", "parallel", "arbitrary"),
        name="diff_attention",
    )(lam_params, qkv, qkv, qkv, sub_g)


def _mla_norm_body(d_ref, qg_ref, kvg_ref, cos_ref, sin_ref, cq_ref, ckv_ref, kr_ref, *, q_rank, kv_rank):
    d = d_ref[0]

    def rms(x, g):
        ms = jnp.mean(x * x, axis=-1, keepdims=True)
        return (x * lax.rsqrt(ms + RMS_EPS)) * g

    cq_ref[0] = rms(d[:, :q_rank], qg_ref[...]).astype(cq_ref.dtype)
    ckv_ref[0] = rms(d[:, q_rank:q_rank + kv_rank], kvg_ref[...]).astype(ckv_ref.dtype)
    kr = d[:, q_rank + kv_rank:q_rank + kv_rank + LANES]
    kr_ref[0] = _rope(kr, cos_ref[...], sin_ref[...]).astype(kr_ref.dtype)


def _mla_norm(down, q_g, kv_g, cos, sin):
    B, T, W = down.shape
    q_rank = q_g.shape[1]
    kv_rank = kv_g.shape[1]
    row = lambda w: pl.BlockSpec((1, ROW_TILE, w), lambda b, t: (b, t, 0))
    full = lambda w: pl.BlockSpec((1, w), lambda b, t: (0, 0))
    tab = pl.BlockSpec((None, ROW_TILE, LANES), lambda b, t: (TAB_ROT, t, 0))
    return pl.pallas_call(
        functools.partial(_mla_norm_body, q_rank=q_rank, kv_rank=kv_rank),
        grid=(B, T // ROW_TILE),
        in_specs=[row(W), full(q_rank), full(kv_rank), tab, tab],
        out_specs=[row(q_rank), row(kv_rank), row(LANES)],
        out_shape=[jax.ShapeDtypeStruct((B, T, q_rank), BF16),
                   jax.ShapeDtypeStruct((B, T, kv_rank), BF16),
                   jax.ShapeDtypeStruct((B, T, LANES), BF16)],
        compiler_params=_params("parallel", "parallel"),
        name="mla_norm_rope",
    )(down, q_g, kv_g, cos, sin)


def _mla_body(qn_ref, qr_ref, kn_ref, kr_ref, v_ref, o_ref, s_ref, *, heads, n_ctx_tiles, n_ctx):
    lane = lax.broadcasted_iota(jnp.int32, (TQ, LANES), 1)

    def run(chunks):
        items = []
        for j in range(heads):
            cols = slice(j * HEAD_V, (j + 1) * HEAD_V)
            pair = slice((j // 2) * LANES, (j // 2 + 1) * LANES)
            qr = qr_ref[0, :, pair]
            mine = (lane // MLA_ROPE) == (j % 2)
            q = jnp.concatenate([qn_ref[0, :, cols], jnp.where(mine, qr, jnp.zeros_like(qr))], axis=1)

            def k_get(lo, n, cols=cols):
                return jnp.concatenate([kn_ref[0, lo:lo + n, cols], kr_ref[0, lo:lo + n, :]], axis=1)

            def v_get(lo, n, cols=cols):
                return v_ref[0, lo:lo + n, cols]

            def emit(o, cols=cols):
                o_ref[0, :, cols] = o.astype(o_ref.dtype)

            items += _problem_items(q, k_get, v_get, chunks, emit)
        _flash(items, s_ref)

    _for_ctx_and_latent_tiles(pl.program_id(2), n_ctx_tiles, n_ctx, kn_ref.shape[1], run)


def _mla_attention(q, kv, kr, n_ctx):
    B, T, Wq = q.shape
    H = Wq // (MLA_NOPE + MLA_ROPE)
    nq = T // TQ
    G = math.gcd(H, MLA_HEADS_PER_STEP)
    assert G % 2 == 0
    W = G * HEAD_V
    Wr = G * MLA_ROPE
    return pl.pallas_call(
        functools.partial(_mla_body, heads=G, n_ctx_tiles=n_ctx // TQ, n_ctx=n_ctx),
        grid=(B, H // G, nq),
        in_specs=[pl.BlockSpec((1, TQ, W), lambda b, g, i: (b, i, g)),
                  pl.BlockSpec((1, TQ, Wr), lambda b, g, i: (b, i, (H * MLA_NOPE) // Wr + g)),
                  pl.BlockSpec((1, T, W), lambda b, g, i: (b, 0, g)),
                  pl.BlockSpec((1, T, LANES), lambda b, g, i: (b, 0, 0)),
                  pl.BlockSpec((1, T, W), lambda b, g, i: (b, 0, H // G + g))],
        out_specs=pl.BlockSpec((1, TQ, W), lambda b, g, i: (b, i, g)),
        out_shape=jax.ShapeDtypeStruct((B, T, H * HEAD_V), BF16),
        scratch_shapes=[_score_scratch()],
        compiler_params=_params("parallel", "parallel", "arbitrary"),
        name="mla_attention",
    )(q, q, kv, kr, kv)


def _na_key_row_start(i, n_rows):
    return jnp.clip(NA_TILE_ROWS * i - NA_WIN_ROWS // 2, 0, n_rows - NA_KEY_ROWS)


def _na_body(q_ref, k_ref, v_ref, bias_ref, o_ref, s_ref, *, heads, n_ctx_tiles, n_ctx, n_rows):
    qi = pl.program_id(2)
    n_lat = NA_KEY_ROWS * GRID_W

    def run(latent):
        start = pl.multiple_of(n_ctx + _na_key_row_start(qi - n_ctx_tiles, n_rows) * GRID_W, GRID_W)
        items = []
        for j in range(heads):
            cols = slice(j * HEAD_V, (j + 1) * HEAD_V)

            def emit(o, cols=cols):
                o_ref[0, :, cols] = o.astype(o_ref.dtype)

            ctx_item = dict(q=q_ref[0, :, cols], n=n_ctx, first=True, last=not latent, emit=emit,
                            k=lambda cols=cols: k_ref[0, :n_ctx, cols],
                            v=lambda cols=cols: v_ref[0, :n_ctx, cols])
            items.append(ctx_item)
            if latent:
                items.append(dict(q=q_ref[0, :, cols], n=n_lat, first=False, last=True, emit=emit,
                                  k=lambda cols=cols: k_ref[0, pl.ds(start, n_lat), cols],
                                  v=lambda cols=cols: v_ref[0, pl.ds(start, n_lat), cols],
                                  bias=lambda j=j: bias_ref[j]))
        _flash(items, s_ref, HEAD_V ** -0.5, pre_scaled=True)

    @pl.when(qi < n_ctx_tiles)
    def _():
        run(False)

    @pl.when(qi >= n_ctx_tiles)
    def _():
        run(True)


def _na_bias_table(rpb, n_rows):
    H = rpb.shape[0]
    n_dr = 2 * NA_WIN_ROWS - 1
    W = GRID_W
    off_idx = np.clip(np.arange(2 * W - 1) - (W - 1) + NA_WIN_COLS - 1, 0, 2 * NA_WIN_COLS - 2)
    ext = rpb[:, :, off_idx]
    ext = jnp.broadcast_to(ext[:, :, None, :], (H, n_dr, W, 2 * W - 1))
    ext = jnp.pad(ext, ((0, 0), (0, 0), (0, 0), (0, 1))).reshape(H, n_dr, W * 2 * W)
    skew = ext[:, :, W - 1:W - 1 + W * (2 * W - 1)].reshape(H, n_dr, W, 2 * W - 1)[..., :W]
    col = np.arange(W)
    col_start = np.clip(col - NA_WIN_COLS // 2, 0, W - NA_WIN_COLS)
    col_in = (col[None, :] >= col_start[:, None]) & (col[None, :] < col_start[:, None] + NA_WIN_COLS)
    col_bias = jnp.where(col_in[None, None], skew, MASK_BIAS)
    col_bias = jnp.concatenate([col_bias, jnp.full((H, 1, W, W), MASK_BIAS, F32)], axis=1)
    n_tiles = n_rows // NA_TILE_ROWS
    tiles = np.array([0, 1, n_tiles - 1])
    ks = np.clip(NA_TILE_ROWS * tiles - NA_WIN_ROWS // 2, 0, n_rows - NA_KEY_ROWS)
    qrow = NA_TILE_ROWS * tiles[:, None] + np.arange(NA_TILE_ROWS)[None]
    krow = ks[:, None] + np.arange(NA_KEY_ROWS)[None]
    r0 = np.clip(qrow - NA_WIN_ROWS // 2, 0, n_rows - NA_WIN_ROWS)
    row_in = (krow[:, None, :] >= r0[:, :, None]) & (krow[:, None, :] < r0[:, :, None] + NA_WIN_ROWS)
    row_idx = np.where(row_in, krow[:, None, :] - qrow[:, :, None] + NA_WIN_ROWS - 1, n_dr)
    col_bias_t = jnp.swapaxes(col_bias, 2, 3)
    blocks = [jnp.concatenate([col_bias_t[:, int(row_idx[v, j, r])] for j in range(NA_TILE_ROWS)], axis=-1)
              for v in range(3) for r in range(NA_KEY_ROWS)]
    return jnp.stack(blocks, axis=1).reshape(H, 3, NA_KEY_ROWS * W, TQ)


def _na_attention(qkv, bias, n_ctx):
    B, T, W3 = qkv.shape
    H = W3 // (3 * HEAD_V)
    G = math.gcd(H, NA_HEADS_PER_STEP)
    W = G * HEAD_V
    nq = T // TQ
    n_ctx_tiles = n_ctx // TQ
    n_rows = (T - n_ctx) // GRID_W
    n_lat_tiles = nq - n_ctx_tiles

    def bias_index(b, g, i):
        t = i - n_ctx_tiles
        return (g, jnp.where(t <= 0, 0, jnp.where(t == n_lat_tiles - 1, 2, 1)), 0, 0)

    return pl.pallas_call(
        functools.partial(_na_body, heads=G, n_ctx_tiles=n_ctx_tiles, n_ctx=n_ctx, n_rows=n_rows),
        grid=(B, H // G, nq),
        in_specs=[pl.BlockSpec((1, TQ, W), lambda b, g, i: (b, i, g)),
                  pl.BlockSpec((1, T, W), lambda b, g, i: (b, 0, H // G + g)),
                  pl.BlockSpec((1, T, W), lambda b, g, i: (b, 0, 2 * (H // G) + g)),
                  pl.BlockSpec((G, None, NA_KEY_ROWS * GRID_W, TQ), bias_index)],
        out_specs=pl.BlockSpec((1, TQ, W), lambda b, g, i: (b, i, g)),
        out_shape=jax.ShapeDtypeStruct((B, T, H * HEAD_V), BF16),
        scratch_shapes=[_score_scratch()],
        compiler_params=_params("parallel", "parallel", "arbitrary"),
        name="neighbourhood_attention",
    )(qkv, qkv, qkv, bias)


def _rope_tables(n_ctx, seq, q_scale):
    pos = jnp.arange(seq)
    quarter = DA_QK_DIM // 4
    inv = ROPE_BASE ** (-jnp.arange(quarter, dtype=F32) / quarter)
    ang_r = (pos // GRID_W).astype(F32)[:, None] * inv
    ang_c = (pos % GRID_W).astype(F32)[:, None] * inv
    cos = jnp.concatenate([jnp.cos(ang_r)] * 2 + [jnp.cos(ang_c)] * 2, axis=1)
    sin = jnp.concatenate([-jnp.sin(ang_r), jnp.sin(ang_r), -jnp.sin(ang_c), jnp.sin(ang_c)], axis=1)
    cos = jnp.concatenate([jnp.ones((n_ctx, DA_QK_DIM), F32), cos], axis=0)
    sin = jnp.concatenate([jnp.zeros((n_ctx, DA_QK_DIM), F32), sin], axis=0)
    cos = jnp.concatenate([cos, cos], axis=1)
    sin = jnp.concatenate([sin, sin], axis=1)
    one, zero = jnp.ones_like(cos), jnp.zeros_like(sin)
    return (jnp.stack([cos * q_scale, cos, one, one * q_scale]),
            jnp.stack([sin * q_scale, sin, zero, zero]))


def kernel(x, c, ctx, c_ctx, ada_w1, ada_w2, ada_b, ln_g, ln_b, mlp_w1, mlp_w2, da_w_qkv, da_w_o, da_lq1,
           da_lk1, da_lq2, da_lk2, da_sub_g, mla_w_down, mla_q_g, mla_w_uq, mla_kv_g, mla_w_ukv, mla_w_o,
           na_w_qkv, na_w_o, na_rpb):
    B, S, D = x.shape
    C = ctx.shape[1]
    T = C + S
    M = B * T
    depth = ada_w1.shape[0]
    alpha = (2.0 * depth) ** 0.25
    n_ctx_tiles = C // ROW_TILE
    assert C % TQ == 0 and S % KEY_CHUNK == 0 and ROW_TILE == TQ and B + 1 <= 8
    assert C <= KEY_CHUNK and NA_KEY_ROWS * GRID_W <= KEY_CHUNK

    da_tabs = _rope_tables(C, S, (DA_QK_DIM ** -0.5) * LOG2E)
    mla_tabs = _rope_tables(C, S, ((MLA_NOPE + MLA_ROPE) ** -0.5) * LOG2E)
    xs = jnp.concatenate([ctx, x], axis=1)
    cond = jnp.concatenate([c, c_ctx[None], jnp.zeros((8 - B - 1, D), F32)], axis=0)
    mods = [_modulation(cond, ada_w1, ada_w2, ada_b, i) for i in range(depth)]

    h = _modulate(xs, mods[0], n_ctx_tiles, scale_idx=1, shift_idx=0)
    for i in range(depth):
        kind, slot = i % N_MIXERS, i // N_MIXERS
        last = i == depth - 1
        h2 = h.reshape(M, D)
        if kind == 0:
            lam_init = 0.8 - 0.6 * math.exp(-0.3 * i)
            n_heads = da_w_o.shape[1] // HEAD_V
            q_cols = n_heads * HEAD_V
            qkv = _matmul(h2, da_w_qkv, slot, BF16, name="da_qkv",
                          rope=(*da_tabs, T, [(0, q_cols, TAB_ROT_SCALED), (q_cols, 2 * q_cols, TAB_ROT)]))
            lam_params = jnp.zeros((8, LANES), F32).at[:4, :DA_QK_DIM].set(
                jnp.stack([da_lq1[slot], da_lk1[slot], da_lq2[slot], da_lk2[slot]]))
            o = _diff_attention(qkv.reshape(B, T, -1), lam_params, da_sub_g[slot][None], C, lam_init)
            w_o = da_w_o
        elif kind == 1:
            w_down = mla_w_down[slot]
            q_rank = mla_q_g.shape[1]
            kv_rank = mla_kv_g.shape[1]
            n_heads = mla_w_o.shape[1] // HEAD_V
            k_rope_w = w_down[:, q_rank + kv_rank:]
            w_down = jnp.concatenate([w_down, k_rope_w, jnp.zeros((D, LANES), F32)], axis=1)
            down = _matmul(h2, w_down[None], 0, F32, name="mla_down")
            cq, ckv, kr = _mla_norm(down.reshape(B, T, -1), mla_q_g[slot][None], mla_kv_g[slot][None], *mla_tabs)
            w_uq = mla_w_uq[slot].reshape(q_rank, n_heads, MLA_NOPE + MLA_ROPE)
            w_uq = jnp.concatenate([w_uq[:, :, :MLA_NOPE].reshape(q_rank, -1),
                                    w_uq[:, :, MLA_NOPE:].reshape(q_rank, -1)], axis=1)
            nope_cols = n_heads * MLA_NOPE
            q = _matmul(cq.reshape(M, q_rank), w_uq[None], 0, BF16,
                        rope=(*mla_tabs, T, [(0, nope_cols, TAB_IDENT_SCALED),
                                             (nope_cols, nope_cols + n_heads * MLA_ROPE, TAB_ROT_SCALED)]),
                        name="mla_uq")
            w_ukv = mla_w_ukv[slot].reshape(kv_rank, n_heads, MLA_NOPE + HEAD_V)
            w_ukv = jnp.concatenate([w_ukv[:, :, :MLA_NOPE].reshape(kv_rank, -1),
                                     w_ukv[:, :, MLA_NOPE:].reshape(kv_rank, -1)], axis=1)
            kv = _matmul(ckv.reshape(M, kv_rank), w_ukv[None], 0, BF16, name="mla_ukv")
            o = _mla_attention(q.reshape(B, T, -1), kv.reshape(B, T, -1), kr, C)
            w_o = mla_w_o
        else:
            qkv = _matmul(h2, na_w_qkv, slot, BF16, name="na_qkv")
            bias = _na_bias_table(na_rpb[slot], S // GRID_W)
            o = _na_attention(qkv.reshape(B, T, -1), bias, C)
            w_o = na_w_o
        if last and o.shape[1] == T:
            o = o[:, C:]
        rows = o.shape[1]
        y = _matmul(o.reshape(B * rows, -1), w_o, slot, F32, name="attn_out").reshape(B, rows, D)
        xs, h = _ln_mod(xs, y, mods[i], ln_g, ln_b, mods[i], n_ctx_tiles, layer=i, alpha=alpha, gate_idx=2,
                        ln_idx=0, scale_idx=4, shift_idx=3)
        u = _matmul(h.reshape(B * rows, D), mlp_w1, i, BF16, epilogue=_epi_relu2, name="mlp_up")
        y = _matmul(u, mlp_w2, i, F32, name="mlp_down").reshape(B, rows, D)
        if not last:
            xs, h = _ln_mod(xs, y, mods[i], ln_g, ln_b, mods[i + 1], n_ctx_tiles, layer=i, alpha=alpha,
                            gate_idx=5, ln_idx=1, scale_idx=1, shift_idx=0)
        else:
            xs, _ = _ln_mod(xs, y, mods[i], ln_g, ln_b, None, 0, layer=i, alpha=alpha, gate_idx=5, ln_idx=1)
    return xs
```

```python
import functools
import math

import numpy as np
import jax
import jax.numpy as jnp
from jax import lax
from jax.experimental import pallas as pl
from jax.experimental.pallas import tpu as pltpu

F32 = jnp.float32
BF16 = jnp.bfloat16

GRID_W = 64
N_MOD = 6
ROPE_BASE = 10000.0
LN_EPS = 1e-5
RMS_EPS = 1e-6
DA_SUB_EPS = 1e-5
DA_QK_DIM = 64
HEAD_V = 128
MLA_NOPE = 128
MLA_ROPE = 64
NA_WIN_ROWS = 8
NA_WIN_COLS = 16
N_MIXERS = 3
MASK_BIAS = -1e30
LOG2E = 1.4426950408889634
TAB_ROT_SCALED, TAB_ROT, TAB_IDENT, TAB_IDENT_SCALED = 0, 1, 2, 3

LANES = 128
MXU_DIM = 256
VMEM_LIMIT_BYTES = 56 * 1024 * 1024

TQ = 256
KEY_CHUNK = 8 * MXU_DIM
FLASH_LOOKAHEAD = 3
DA_HEADS_PER_STEP = 4
MLA_HEADS_PER_STEP = 4
NA_HEADS_PER_STEP = 4
NA_TILE_ROWS = TQ // GRID_W
NA_KEY_ROWS = 12
ROW_TILE = 256
MM_TM = 544
MM_TN = 512
MM_TN_WHOLE_K = 1024
MM_TK_FULL = 4096
MM_TK = 4096
MM_ROW_SPLITS = 4


def _pick_tile(dim, target, align):
    best = None
    for t in range(align, min(dim, target) + 1, align):
        if dim % t == 0:
            best = t
    return best if best is not None else dim


def _params(*sem):
    return pltpu.CompilerParams(dimension_semantics=sem, vmem_limit_bytes=VMEM_LIMIT_BYTES)


def _rope(x, cos, sin):
    n = x.shape[1]
    reps = n // LANES
    if reps > 1:
        cos = jnp.concatenate([cos] * reps, axis=1)
        sin = jnp.concatenate([sin] * reps, axis=1)
    lane = lax.broadcasted_iota(jnp.int32, x.shape, 1)
    nxt = pltpu.roll(x, n - 16, 1)
    prv = pltpu.roll(x, 16, 1)
    partner = jnp.where((lane & 16) == 0, nxt, prv)
    return x * cos + partner * sin


def _mm_body(*refs, nk, epilogue, n_extra, splits):
    x_ref, w_ref = refs[0], refs[1]
    extra = refs[2:2 + n_extra]
    o_ref = refs[2 + n_extra]
    wb_ref = refs[3 + n_extra]
    k = pl.program_id(1)
    i = pl.program_id(2)

    @pl.when(i == 0)
    def _():
        wb_ref[...] = w_ref[...].astype(BF16)

    if nk > 1:
        acc_ref = refs[4 + n_extra]

        @pl.when((pl.program_id(0) == 0) & (k == 0))
        def _():
            acc_ref[i] = jnp.zeros(acc_ref.shape[1:], F32)

    tm = x_ref.shape[0]
    sub = tm // splits
    for r in range(splits):
        rows = slice(r * sub, (r + 1) * sub)
        acc = jnp.dot(x_ref[rows, :], wb_ref[...], preferred_element_type=F32)
        if nk > 1:
            acc = jnp.where(k > 0, acc_ref[i, rows, :], 0.0) + acc
            acc_ref[i, rows, :] = acc
        epilogue(acc, rows, o_ref, *extra)


def _epi_identity(acc, rows, o_ref):
    o_ref[rows, :] = acc.astype(o_ref.dtype)


def _epi_relu2(acc, rows, o_ref):
    r = jnp.maximum(acc, 0.0)
    o_ref[rows, :] = (r * r).astype(o_ref.dtype)


def _epi_rope(acc, rows, o_ref, cos_ref, sin_ref):
    o_ref[rows, :] = _rope(acc, cos_ref[rows, :], sin_ref[rows, :]).astype(o_ref.dtype)


def _matmul(x, w, layer, out_dtype, *, epilogue=_epi_identity, rope=None, name="matmul"):
    M, K = x.shape
    N = w.shape[2]
    tk = K if K <= MM_TK_FULL else _pick_tile(K, MM_TK, LANES)
    nk = K // tk
    tm_target, tn_target = MM_TM, (MM_TN if nk > 1 else MM_TN_WHOLE_K)
    if rope is None:
        tm = _pick_tile(M, tm_target, 16)
        tn = _pick_tile(N, tn_target, LANES)
    else:
        tm = _pick_tile(rope[2], tm_target, 16)
        tn = _pick_tile(math.gcd(N, *[b for lo, hi, _ in rope[3] for b in (lo, hi)]), tn_target, LANES)
    nm = M // tm
    last = nk - 1
    in_specs = [pl.BlockSpec((tm, tk), lambda j, k, i: (i, k)),
                pl.BlockSpec((None, tk, tn), lambda j, k, i: (layer, k, j))]
    operands = [x, w]
    n_extra = 0
    if rope is not None:
        cos, sin, rows_per_batch, ranges = rope
        per = rows_per_batch // tm

        def table_set(j):
            which = TAB_IDENT
            for lo, hi, tab in ranges:
                which = jnp.where((j * tn >= lo) & (j * tn < hi), tab, which)
            return which

        tab_spec = pl.BlockSpec((None, tm, LANES), lambda j, k, i: (table_set(j), i % per, 0))
        in_specs += [tab_spec, tab_spec]
        operands += [cos, sin]
        n_extra = 2
        epilogue = _epi_rope
    scratch = [pltpu.VMEM((tk, tn), BF16)]
    if nk > 1:
        scratch.append(pltpu.VMEM((nm, tm, tn), F32))
    out_spec = pl.BlockSpec((tm, tn), lambda j, k, i: (jnp.where(k == last, i, 0), j))
    return pl.pallas_call(
        functools.partial(_mm_body, nk=nk, epilogue=epilogue, n_extra=n_extra,
                          splits=max(s for s in (MM_ROW_SPLITS, 2, 1) if tm % (16 * s) == 0)),
        grid=(N // tn, nk, nm),
        in_specs=in_specs,
        out_specs=out_spec,
        out_shape=jax.ShapeDtypeStruct((M, N), out_dtype),
        scratch_shapes=scratch,
        compiler_params=_params("parallel", "arbitrary", "arbitrary"),
        name=name,
    )(*operands)


def _mod_body(c_ref, w1_ref, w2_ref, b_ref, o_ref):
    c = c_ref[...]
    s = c * (1.0 / (1.0 + jnp.exp(-c)))
    h = jnp.dot(s.astype(BF16), w1_ref[...].astype(BF16), preferred_element_type=F32)
    o = jnp.dot(h.astype(BF16), w2_ref[...].astype(BF16), preferred_element_type=F32)
    o_ref[...] = o + b_ref[...]


def _modulation(cond, w1, w2, b, layer):
    R, D = cond.shape
    rank = w1.shape[2]
    N = w2.shape[2]
    tn = _pick_tile(N, 2048, LANES)
    out = pl.pallas_call(
        _mod_body,
        grid=(N // tn,),
        in_specs=[pl.BlockSpec((R, D), lambda j: (0, 0)),
                  pl.BlockSpec((None, D, rank), lambda j: (layer, 0, 0)),
                  pl.BlockSpec((None, rank, tn), lambda j: (layer, 0, j)),
                  pl.BlockSpec((None, 1, tn), lambda j: (layer, 0, j))],
        out_specs=pl.BlockSpec((R, tn), lambda j: (0, j)),
        out_shape=jax.ShapeDtypeStruct((R, N), F32),
        compiler_params=_params("arbitrary"),
        name="adaln_modulation",
    )(cond, w1, w2, b.reshape(b.shape[0], 1, N))
    return out.reshape(R, N_MOD, D)


def _modulate_body(x_ref, mod_ref, h_ref, *, scale_idx, shift_idx):
    x = x_ref[0]
    scale = mod_ref[0, scale_idx:scale_idx + 1, :]
    shift = mod_ref[0, shift_idx:shift_idx + 1, :]
    h_ref[0] = (x * (1.0 + scale) + shift).astype(h_ref.dtype)


def _modulate(x, mod, n_ctx_tiles, *, scale_idx, shift_idx):
    B, T, D = x.shape
    ctx_row = B
    return pl.pallas_call(
        functools.partial(_modulate_body, scale_idx=scale_idx, shift_idx=shift_idx),
        grid=(B, T // ROW_TILE),
        in_specs=[pl.BlockSpec((1, ROW_TILE, D), lambda b, t: (b, t, 0)),
                  pl.BlockSpec((1, N_MOD, D), lambda b, t: (jnp.where(t < n_ctx_tiles, ctx_row, b), 0, 0))],
        out_specs=pl.BlockSpec((1, ROW_TILE, D), lambda b, t: (b, t, 0)),
        out_shape=jax.ShapeDtypeStruct((B, T, D), BF16),
        compiler_params=_params("parallel", "parallel"),
        name="modulate",
    )(x, mod)


def _ln_mod_body(x_ref, y_ref, modg_ref, lng_ref, lnb_ref, *rest, alpha, gate_idx, ln_idx, scale_idx,
                 shift_idx, want_h):
    if want_h:
        modn_ref, xo_ref, ho_ref = rest
    else:
        (xo_ref,) = rest
    x = x_ref[0]
    y = y_ref[0]
    gate = modg_ref[0, gate_idx:gate_idx + 1, :]
    z = alpha * x + gate * y
    mu = jnp.mean(z, axis=-1, keepdims=True)
    zc = z - mu
    var = jnp.mean(zc * zc, axis=-1, keepdims=True)
    xn = (zc * lax.rsqrt(var + LN_EPS)) * lng_ref[ln_idx:ln_idx + 1, :] + lnb_ref[ln_idx:ln_idx + 1, :]
    xo_ref[0] = xn
    if want_h:
        scale = modn_ref[0, scale_idx:scale_idx + 1, :]
        shift = modn_ref[0, shift_idx:shift_idx + 1, :]
        ho_ref[0] = (xn * (1.0 + scale) + shift).astype(ho_ref.dtype)


def _ln_mod(x, y, mod_gate, ln_g, ln_b, mod_next, n_ctx_tiles, *, layer, alpha, gate_idx, ln_idx, scale_idx=0,
            shift_idx=0):
    B, T, D = x.shape
    ctx_row = B
    want_h = mod_next is not None
    off = (T - y.shape[1]) // ROW_TILE
    assert off in (0, n_ctx_tiles)
    n_tiles = T // ROW_TILE - off
    x_spec = pl.BlockSpec((1, ROW_TILE, D), lambda b, t: (b, t + off, 0))
    row_out = pl.BlockSpec((1, ROW_TILE, D), lambda b, t: (b, t, 0))
    mod_spec = pl.BlockSpec((1, N_MOD, D), lambda b, t: (jnp.where(t + off < n_ctx_tiles, ctx_row, b), 0, 0))
    ln_spec = pl.BlockSpec((None, 2, D), lambda b, t: (layer, 0, 0))
    in_specs = [x_spec, row_out, mod_spec, ln_spec, ln_spec]
    operands = [x, y, mod_gate, ln_g, ln_b]
    out_specs = [row_out]
    out_shape = [jax.ShapeDtypeStruct((B, n_tiles * ROW_TILE, D), F32)]
    if want_h:
        in_specs.append(mod_spec)
        operands.append(mod_next)
        out_specs.append(row_out)
        out_shape.append(jax.ShapeDtypeStruct((B, n_tiles * ROW_TILE, D), BF16))
    res = pl.pallas_call(
        functools.partial(_ln_mod_body, alpha=alpha, gate_idx=gate_idx, ln_idx=ln_idx, scale_idx=scale_idx,
                          shift_idx=shift_idx, want_h=want_h),
        grid=(B, n_tiles),
        in_specs=in_specs,
        out_specs=out_specs,
        out_shape=out_shape,
        compiler_params=_params("parallel", "parallel"),
        name="resid_ln_modulate",
    )(*operands)
    return (res[0], res[1]) if want_h else (res[0], None)


def _flash(items, s_ref, scale=None, pre_scaled=False):
    c = LOG2E if pre_scaled else (None if scale is None else scale * LOG2E)

    def exp2_scaled(d):
        return jnp.exp2(d if c is None else d * c)

    def put(i):
        it = items[i]
        s = lax.dot_general(it["k"](), it["q"], (((1,), (1,)), ((), ())), preferred_element_type=F32)
        s_ref[i % FLASH_LOOKAHEAD, :it["n"], :] = s

    def pv(it, p):
        return lax.dot_general(it["v"](), p.astype(BF16), (((0,), (0,)), ((), ())), preferred_element_type=F32)

    for i in range(min(FLASH_LOOKAHEAD, len(items))):
        put(i)
    state = None
    for idx, it in enumerate(items):
        s = s_ref[idx % FLASH_LOOKAHEAD, :it["n"], :]
        if idx + FLASH_LOOKAHEAD < len(items):
            put(idx + FLASH_LOOKAHEAD)
        if pre_scaled:
            s = s * scale
            if it.get("bias") is not None:
                s = s + it["bias"]()
        cm = jnp.max(s, axis=0, keepdims=True)
        if it["first"]:
            m = cm
            p = exp2_scaled(s - m)
            l = jnp.sum(p, axis=0, keepdims=True)
            acc = pv(it, p)
        else:
            m_old, l_old, acc_old = state
            m = jnp.maximum(m_old, cm)
            alpha = exp2_scaled(m_old - m)
            p = exp2_scaled(s - m)
            l = alpha * l_old + jnp.sum(p, axis=0, keepdims=True)
            acc = alpha * acc_old + pv(it, p)
        state = (m, l, acc)
        if it["last"]:
            it["emit"]((acc / l).T)


def _score_scratch():
    return pltpu.VMEM((FLASH_LOOKAHEAD, KEY_CHUNK, TQ), F32)


def _key_chunks(n_ctx, n_keys):
    return [(0, n_ctx)] + [(lo, KEY_CHUNK) for lo in range(n_ctx, n_keys, KEY_CHUNK)]


def _problem_items(q, k_get, v_get, chunks, emit):
    return [dict(q=q, k=functools.partial(k_get, lo, n), v=functools.partial(v_get, lo, n), n=n,
                 first=ci == 0, last=ci == len(chunks) - 1, emit=emit)
            for ci, (lo, n) in enumerate(chunks)]


def _for_ctx_and_latent_tiles(qi, n_ctx_tiles, n_ctx, n_keys, run):
    @pl.when(qi < n_ctx_tiles)
    def _():
        run(_key_chunks(n_ctx, n_ctx))

    @pl.when(qi >= n_ctx_tiles)
    def _():
        run(_key_chunks(n_ctx, n_keys))


def _da_body(lam_ref, q_ref, k_ref, v_ref, g_ref, o_ref, s_ref, *, heads, n_ctx_tiles, n_ctx, lam_init):
    lp = lam_ref[...]
    lam = (jnp.exp(jnp.sum(lp[0:1] * lp[1:2], axis=-1, keepdims=True))
           - jnp.exp(jnp.sum(lp[2:3] * lp[3:4], axis=-1, keepdims=True)) + lam_init)
    lane = lax.broadcasted_iota(jnp.int32, (TQ, HEAD_V), 1)

    def run(chunks):
        items = []
        for j in range(heads):
            cols = slice(j * HEAD_V, (j + 1) * HEAD_V)
            q = q_ref[0, :, cols]
            zero = jnp.zeros_like(q)
            comps = [jnp.where(lane < DA_QK_DIM, q, zero), jnp.where(lane >= DA_QK_DIM, q, zero)]
            outs = []

            def finish(o1, cols=cols, outs=outs):
                o = outs[0] - lam * o1
                ms = jnp.mean(o * o, axis=-1, keepdims=True)
                on = (o * lax.rsqrt(ms + DA_SUB_EPS)) * g_ref[...] * (1.0 - lam_init)
                o_ref[0, :, cols] = on.astype(o_ref.dtype)

            def k_get(lo, n, cols=cols):
                return k_ref[0, lo:lo + n, cols]

            def v_get(lo, n, cols=cols):
                return v_ref[0, lo:lo + n, cols]

            items += _problem_items(comps[0], k_get, v_get, chunks, outs.append)
            items += _problem_items(comps[1], k_get, v_get, chunks, finish)
        _flash(items, s_ref)

    _for_ctx_and_latent_tiles(pl.program_id(2), n_ctx_tiles, n_ctx, k_ref.shape[1], run)


def _diff_attention(qkv, lam_params, sub_g, n_ctx, lam_init):
    B, T, W3 = qkv.shape
    H = W3 // (3 * HEAD_V)
    G = math.gcd(H, DA_HEADS_PER_STEP)
    W = G * HEAD_V
    nq = T // TQ
    return pl.pallas_call(
        functools.partial(_da_body, heads=G, n_ctx_tiles=n_ctx // TQ, n_ctx=n_ctx, lam_init=lam_init),
        grid=(B, H // G, nq),
        in_specs=[pl.BlockSpec((8, LANES), lambda b, g, i: (0, 0)),
                  pl.BlockSpec((1, TQ, W), lambda b, g, i: (b, i, g)),
                  pl.BlockSpec((1, T, W), lambda b, g, i: (b, 0, H // G + g)),
                  pl.BlockSpec((1, T, W), lambda b, g, i: (b, 0, 2 * (H // G) + g)),
                  pl.BlockSpec((1, HEAD_V), lambda b, g, i: (0, 0))],
        out_specs=pl.BlockSpec((1, TQ, W), lambda b, g, i: (b, i, g)),
        out_shape=jax.ShapeDtypeStruct((B, nq * TQ, H * HEAD_V), BF16),
        scratch_shapes=[_score_scratch()],
        compiler_params=_params("parallel", "parallel", "arbitrary"),
        name="diff_attention",
    )(lam_params, qkv, qkv, qkv, sub_g)


def _mla_norm_body(d_ref, qg_ref, kvg_ref, cos_ref, sin_ref, cq_ref, ckv_ref, kr_ref, *, q_rank, kv_rank):
    d = d_ref[0]

    def rms(x, g):
        ms = jnp.mean(x * x, axis=-1, keepdims=True)
        return (x * lax.rsqrt(ms + RMS_EPS)) * g

    cq_ref[0] = rms(d[:, :q_rank], qg_ref[...]).astype(cq_ref.dtype)
    ckv_ref[0] = rms(d[:, q_rank:q_rank + kv_rank], kvg_ref[...]).astype(ckv_ref.dtype)
    kr = d[:, q_rank + kv_rank:q_rank + kv_rank + LANES]
    kr_ref[0] = _rope(kr, cos_ref[...], sin_ref[...]).astype(kr_ref.dtype)


def _mla_norm(down, q_g, kv_g, cos, sin):
    B, T, W = down.shape
    q_rank = q_g.shape[1]
    kv_rank = kv_g.shape[1]
    row = lambda w: pl.BlockSpec((1, ROW_TILE, w), lambda b, t: (b, t, 0))
    full = lambda w: pl.BlockSpec((1, w), lambda b, t: (0, 0))
    tab = pl.BlockSpec((None, ROW_TILE, LANES), lambda b, t: (TAB_ROT, t, 0))
    return pl.pallas_call(
        functools.partial(_mla_norm_body, q_rank=q_rank, kv_rank=kv_rank),
        grid=(B, T // ROW_TILE),
        in_specs=[row(W), full(q_rank), full(kv_rank), tab, tab],
        out_specs=[row(q_rank), row(kv_rank), row(LANES)],
        out_shape=[jax.ShapeDtypeStruct((B, T, q_rank), BF16),
                   jax.ShapeDtypeStruct((B, T, kv_rank), BF16),
                   jax.ShapeDtypeStruct((B, T, LANES), BF16)],
        compiler_params=_params("parallel", "parallel"),
        name="mla_norm_rope",
    )(down, q_g, kv_g, cos, sin)


def _mla_body(qn_ref, qr_ref, kn_ref, kr_ref, v_ref, o_ref, s_ref, *, heads, n_ctx_tiles, n_ctx):
    lane = lax.broadcasted_iota(jnp.int32, (TQ, LANES), 1)

    def run(chunks):
        items = []
        for j in range(heads):
            cols = slice(j * HEAD_V, (j + 1) * HEAD_V)
            pair = slice((j // 2) * LANES, (j // 2 + 1) * LANES)
            qr = qr_ref[0, :, pair]
            mine = (lane // MLA_ROPE) == (j % 2)
            q = jnp.concatenate([qn_ref[0, :, cols], jnp.where(mine, qr, jnp.zeros_like(qr))], axis=1)

            def k_get(lo, n, cols=cols):
                return jnp.concatenate([kn_ref[0, lo:lo + n, cols], kr_ref[0, lo:lo + n, :]], axis=1)

            def v_get(lo, n, cols=cols):
                return v_ref[0, lo:lo + n, cols]

            def emit(o, cols=cols):
                o_ref[0, :, cols] = o.astype(o_ref.dtype)

            items += _problem_items(q, k_get, v_get, chunks, emit)
        _flash(items, s_ref)

    _for_ctx_and_latent_tiles(pl.program_id(2), n_ctx_tiles, n_ctx, kn_ref.shape[1], run)


def _mla_attention(q, kv, kr, n_ctx):
    B, T, Wq = q.shape
    H = Wq // (MLA_NOPE + MLA_ROPE)
    nq = T // TQ
    G = math.gcd(H, MLA_HEADS_PER_STEP)
    assert G % 2 == 0
    W = G * HEAD_V
    Wr = G * MLA_ROPE
    return pl.pallas_call(
        functools.partial(_mla_body, heads=G, n_ctx_tiles=n_ctx // TQ, n_ctx=n_ctx),
        grid=(B, H // G, nq),
        in_specs=[pl.BlockSpec((1, TQ, W), lambda b, g, i: (b, i, g)),
                  pl.BlockSpec((1, TQ, Wr), lambda b, g, i: (b, i, (H * MLA_NOPE) // Wr + g)),
                  pl.BlockSpec((1, T, W), lambda b, g, i: (b, 0, g)),
                  pl.BlockSpec((1, T, LANES), lambda b, g, i: (b, 0, 0)),
                  pl.BlockSpec((1, T, W), lambda b, g, i: (b, 0, H // G + g))],
        out_specs=pl.BlockSpec((1, TQ, W), lambda b, g, i: (b, i, g)),
        out_shape=jax.ShapeDtypeStruct((B, T, H * HEAD_V), BF16),
        scratch_shapes=[_score_scratch()],
        compiler_params=_params("parallel", "parallel", "arbitrary"),
        name="mla_attention",
    )(q, q, kv, kr, kv)


def _na_key_row_start(i, n_rows):
    return jnp.clip(NA_TILE_ROWS * i - NA_WIN_ROWS // 2, 0, n_rows - NA_KEY_ROWS)


def _na_body(q_ref, k_ref, v_ref, bias_ref, o_ref, s_ref, *, heads, n_ctx_tiles, n_ctx, n_rows):
    qi = pl.program_id(2)
    n_lat = NA_KEY_ROWS * GRID_W

    def run(latent):
        start = pl.multiple_of(n_ctx + _na_key_row_start(qi - n_ctx_tiles, n_rows) * GRID_W, GRID_W)
        items = []
        for j in range(heads):
            cols = slice(j * HEAD_V, (j + 1) * HEAD_V)

            def emit(o, cols=cols):
                o_ref[0, :, cols] = o.astype(o_ref.dtype)

            ctx_item = dict(q=q_ref[0, :, cols], n=n_ctx, first=True, last=not latent, emit=emit,
                            k=lambda cols=cols: k_ref[0, :n_ctx, cols],
                            v=lambda cols=cols: v_ref[0, :n_ctx, cols])
            items.append(ctx_item)
            if latent:
                items.append(dict(q=q_ref[0, :, cols], n=n_lat, first=False, last=True, emit=emit,
                                  k=lambda cols=cols: k_ref[0, pl.ds(start, n_lat), cols],
                                  v=lambda cols=cols: v_ref[0, pl.ds(start, n_lat), cols],
                                  bias=lambda j=j: bias_ref[j]))
        _flash(items, s_ref, HEAD_V ** -0.5, pre_scaled=True)

    @pl.when(qi < n_ctx_tiles)
    def _():
        run(False)

    @pl.when(qi >= n_ctx_tiles)
    def _():
        run(True)


def _na_bias_table(rpb, n_rows):
    H = rpb.shape[0]
    n_dr = 2 * NA_WIN_ROWS - 1
    W = GRID_W
    off_idx = np.clip(np.arange(2 * W - 1) - (W - 1) + NA_WIN_COLS - 1, 0, 2 * NA_WIN_COLS - 2)
    ext = rpb[:, :, off_idx]
    ext = jnp.broadcast_to(ext[:, :, None, :], (H, n_dr, W, 2 * W - 1))
    ext = jnp.pad(ext, ((0, 0), (0, 0), (0, 0), (0, 1))).reshape(H, n_dr, W * 2 * W)
    skew = ext[:, :, W - 1:W - 1 + W * (2 * W - 1)].reshape(H, n_dr, W, 2 * W - 1)[..., :W]
    col = np.arange(W)
    col_start = np.clip(col - NA_WIN_COLS // 2, 0, W - NA_WIN_COLS)
    col_in = (col[None, :] >= col_start[:, None]) & (col[None, :] < col_start[:, None] + NA_WIN_COLS)
    col_bias = jnp.where(col_in[None, None], skew, MASK_BIAS)
    col_bias = jnp.concatenate([col_bias, jnp.full((H, 1, W, W), MASK_BIAS, F32)], axis=1)
    n_tiles = n_rows // NA_TILE_ROWS
    tiles = np.array([0, 1, n_tiles - 1])
    ks = np.clip(NA_TILE_ROWS * tiles - NA_WIN_ROWS // 2, 0, n_rows - NA_KEY_ROWS)
    qrow = NA_TILE_ROWS * tiles[:, None] + np.arange(NA_TILE_ROWS)[None]
    krow = ks[:, None] + np.arange(NA_KEY_ROWS)[None]
    r0 = np.clip(qrow - NA_WIN_ROWS // 2, 0, n_rows - NA_WIN_ROWS)
    row_in = (krow[:, None, :] >= r0[:, :, None]) & (krow[:, None, :] < r0[:, :, None] + NA_WIN_ROWS)
    row_idx = np.where(row_in, krow[:, None, :] - qrow[:, :, None] + NA_WIN_ROWS - 1, n_dr)
    col_bias_t = jnp.swapaxes(col_bias, 2, 3)
    blocks = [jnp.concatenate([col_bias_t[:, int(row_idx[v, j, r])] for j in range(NA_TILE_ROWS)], axis=-1)
              for v in range(3) for r in range(NA_KEY_ROWS)]
    return jnp.stack(blocks, axis=1).reshape(H, 3, NA_KEY_ROWS * W, TQ)


def _na_attention(qkv, bias, n_ctx):
    B, T, W3 = qkv.shape
    H = W3 // (3 * HEAD_V)
    G = math.gcd(H, NA_HEADS_PER_STEP)
    W = G * HEAD_V
    nq = T // TQ
    n_ctx_tiles = n_ctx // TQ
    n_rows = (T - n_ctx) // GRID_W
    n_lat_tiles = nq - n_ctx_tiles

    def bias_index(b, g, i):
        t = i - n_ctx_tiles
        return (g, jnp.where(t <= 0, 0, jnp.where(t == n_lat_tiles - 1, 2, 1)), 0, 0)

    return pl.pallas_call(
        functools.partial(_na_body, heads=G, n_ctx_tiles=n_ctx_tiles, n_ctx=n_ctx, n_rows=n_rows),
        grid=(B, H // G, nq),
        in_specs=[pl.BlockSpec((1, TQ, W), lambda b, g, i: (b, i, g)),
                  pl.BlockSpec((1, T, W), lambda b, g, i: (b, 0, H // G + g)),
                  pl.BlockSpec((1, T, W), lambda b, g, i: (b, 0, 2 * (H // G) + g)),
                  pl.BlockSpec((G, None, NA_KEY_ROWS * GRID_W, TQ), bias_index)],
        out_specs=pl.BlockSpec((1, TQ, W), lambda b, g, i: (b, i, g)),
        out_shape=jax.ShapeDtypeStruct((B, T, H * HEAD_V), BF16),
        scratch_shapes=[_score_scratch()],
        compiler_params=_params("parallel", "parallel", "arbitrary"),
        name="neighbourhood_attention",
    )(qkv, qkv, qkv, bias)


def _rope_tables(n_ctx, seq, q_scale):
    pos = jnp.arange(seq)
    quarter = DA_QK_DIM // 4
    inv = ROPE_BASE ** (-jnp.arange(quarter, dtype=F32) / quarter)
    ang_r = (pos // GRID_W).astype(F32)[:, None] * inv
    ang_c = (pos % GRID_W).astype(F32)[:, None] * inv
    cos = jnp.concatenate([jnp.cos(ang_r)] * 2 + [jnp.cos(ang_c)] * 2, axis=1)
    sin = jnp.concatenate([-jnp.sin(ang_r), jnp.sin(ang_r), -jnp.sin(ang_c), jnp.sin(ang_c)], axis=1)
    cos = jnp.concatenate([jnp.ones((n_ctx, DA_QK_DIM), F32), cos], axis=0)
    sin = jnp.concatenate([jnp.zeros((n_ctx, DA_QK_DIM), F32), sin], axis=0)
    cos = jnp.concatenate([cos, cos], axis=1)
    sin = jnp.concatenate([sin, sin], axis=1)
    one, zero = jnp.ones_like(cos), jnp.zeros_like(sin)
    return (jnp.stack([cos * q_scale, cos, one, one * q_scale]),
            jnp.stack([sin * q_scale, sin, zero, zero]))


def kernel(x, c, ctx, c_ctx, ada_w1, ada_w2, ada_b, ln_g, ln_b, mlp_w1, mlp_w2, da_w_qkv, da_w_o, da_lq1,
           da_lk1, da_lq2, da_lk2, da_sub_g, mla_w_down, mla_q_g, mla_w_uq, mla_kv_g, mla_w_ukv, mla_w_o,
           na_w_qkv, na_w_o, na_rpb):
    B, S, D = x.shape
    C = ctx.shape[1]
    T = C + S
    M = B * T
    depth = ada_w1.shape[0]
    alpha = (2.0 * depth) ** 0.25
    n_ctx_tiles = C // ROW_TILE
    assert C % TQ == 0 and S % KEY_CHUNK == 0 and ROW_TILE == TQ and B + 1 <= 8
    assert C <= KEY_CHUNK and NA_KEY_ROWS * GRID_W <= KEY_CHUNK

    da_tabs = _rope_tables(C, S, (DA_QK_DIM ** -0.5) * LOG2E)
    mla_tabs = _rope_tables(C, S, ((MLA_NOPE + MLA_ROPE) ** -0.5) * LOG2E)
    xs = jnp.concatenate([ctx, x], axis=1)
    cond = jnp.concatenate([c, c_ctx[None], jnp.zeros((8 - B - 1, D), F32)], axis=0)
    mods = [_modulation(cond, ada_w1, ada_w2, ada_b, i) for i in range(depth)]

    h = _modulate(xs, mods[0], n_ctx_tiles, scale_idx=1, shift_idx=0)
    for i in range(depth):
        kind, slot = i % N_MIXERS, i // N_MIXERS
        last = i == depth - 1
        h2 = h.reshape(M, D)
        if kind == 0:
            lam_init = 0.8 - 0.6 * math.exp(-0.3 * i)
            n_heads = da_w_o.shape[1] // HEAD_V
            q_cols = n_heads * HEAD_V
            qkv = _matmul(h2, da_w_qkv, slot, BF16, name="da_qkv",
                          rope=(*da_tabs, T, [(0, q_cols, TAB_ROT_SCALED), (q_cols, 2 * q_cols, TAB_ROT)]))
            lam_params = jnp.zeros((8, LANES), F32).at[:4, :DA_QK_DIM].set(
                jnp.stack([da_lq1[slot], da_lk1[slot], da_lq2[slot], da_lk2[slot]]))
            o = _diff_attention(qkv.reshape(B, T, -1), lam_params, da_sub_g[slot][None], C, lam_init)
            w_o = da_w_o
        elif kind == 1:
            w_down = mla_w_down[slot]
            q_rank = mla_q_g.shape[1]
            kv_rank = mla_kv_g.shape[1]
            n_heads = mla_w_o.shape[1] // HEAD_V
            k_rope_w = w_down[:, q_rank + kv_rank:]
            w_down = jnp.concatenate([w_down, k_rope_w, jnp.zeros((D, LANES), F32)], axis=1)
            down = _matmul(h2, w_down[None], 0, F32, name="mla_down")
            cq, ckv, kr = _mla_norm(down.reshape(B, T, -1), mla_q_g[slot][None], mla_kv_g[slot][None], *mla_tabs)
            w_uq = mla_w_uq[slot].reshape(q_rank, n_heads, MLA_NOPE + MLA_ROPE)
            w_uq = jnp.concatenate([w_uq[:, :, :MLA_NOPE].reshape(q_rank, -1),
                                    w_uq[:, :, MLA_NOPE:].reshape(q_rank, -1)], axis=1)
            nope_cols = n_heads * MLA_NOPE
            q = _matmul(cq.reshape(M, q_rank), w_uq[None], 0, BF16,
                        rope=(*mla_tabs, T, [(0, nope_cols, TAB_IDENT_SCALED),
                                             (nope_cols, nope_cols + n_heads * MLA_ROPE, TAB_ROT_SCALED)]),
                        name="mla_uq")
            w_ukv = mla_w_ukv[slot].reshape(kv_rank, n_heads, MLA_NOPE + HEAD_V)
            w_ukv = jnp.concatenate([w_ukv[:, :, :MLA_NOPE].reshape(kv_rank, -1),
                                     w_ukv[:, :, MLA_NOPE:].reshape(kv_rank, -1)], axis=1)
            kv = _matmul(ckv.reshape(M, kv_rank), w_ukv[None], 0, BF16, name="mla_ukv")
            o = _mla_attention(q.reshape(B, T, -1), kv.reshape(B, T, -1), kr, C)
            w_o = mla_w_o
        else:
            qkv = _matmul(h2, na_w_qkv, slot, BF16, name="na_qkv")
            bias = _na_bias_table(na_rpb[slot], S // GRID_W)
            o = _na_attention(qkv.reshape(B, T, -1), bias, C)
            w_o = na_w_o
        if last and o.shape[1] == T:
            o = o[:, C:]
        rows = o.shape[1]
        y = _matmul(o.reshape(B * rows, -1), w_o, slot, F32, name="attn_out").reshape(B, rows, D)
        xs, h = _ln_mod(xs, y, mods[i], ln_g, ln_b, mods[i], n_ctx_tiles, layer=i, alpha=alpha, gate_idx=2,
                        ln_idx=0, scale_idx=4, shift_idx=3)
        u = _matmul(h.reshape(B * rows, D), mlp_w1, i, BF16, epilogue=_epi_relu2, name="mlp_up")
        y = _matmul(u, mlp_w2, i, F32, name="mlp_down").reshape(B, rows, D)
        if not last:
            xs, h = _ln_mod(xs, y, mods[i], ln_g, ln_b, mods[i + 1], n_ctx_tiles, layer=i, alpha=alpha,
                            gate_idx=5, ln_idx=1, scale_idx=1, shift_idx=0)
        else:
            xs, _ = _ln_mod(xs, y, mods[i], ln_g, ln_b, None, 0, layer=i, alpha=alpha, gate_idx=5, ln_idx=1)
    return xs
```

```python
import functools
import math

import numpy as np
import jax
import jax.numpy as jnp
from jax import lax
from jax.experimental import pallas as pl
from jax.experimental.pallas import tpu as pltpu

F32 = jnp.float32
BF16 = jnp.bfloat16

GRID_W = 64
N_MOD = 6
ROPE_BASE = 10000.0
LN_EPS = 1e-5
RMS_EPS = 1e-6
DA_SUB_EPS = 1e-5
DA_QK_DIM = 64
HEAD_V = 128
MLA_NOPE = 128
MLA_ROPE = 64
NA_WIN_ROWS = 8
NA_WIN_COLS = 16
N_MIXERS = 3
MASK_BIAS = -1e30
LOG2E = 1.4426950408889634
TAB_ROT_SCALED, TAB_ROT, TAB_IDENT, TAB_IDENT_SCALED = 0, 1, 2, 3

LANES = 128
MXU_DIM = 256
VMEM_LIMIT_BYTES = 56 * 1024 * 1024

TQ = 256
KEY_CHUNK = 8 * MXU_DIM
FLASH_LOOKAHEAD = 3
DA_HEADS_PER_STEP = 4
MLA_HEADS_PER_STEP = 4
NA_HEADS_PER_STEP = 4
NA_TILE_ROWS = TQ // GRID_W
NA_KEY_ROWS = 12
ROW_TILE = 256
MM_TM = 544
MM_TN = 512
MM_TN_WHOLE_K = 1024
MM_TK_FULL = 4096
MM_TK = 4096
MM_ROW_SPLITS = 4


def _pick_tile(dim, target, align):
    best = None
    for t in range(align, min(dim, target) + 1, align):
        if dim % t == 0:
            best = t
    return best if best is not None else dim


def _params(*sem):
    return pltpu.CompilerParams(dimension_semantics=sem, vmem_limit_bytes=VMEM_LIMIT_BYTES)


def _rope(x, cos, sin):
    n = x.shape[1]
    reps = n // LANES
    if reps > 1:
        cos = jnp.concatenate([cos] * reps, axis=1)
        sin = jnp.concatenate([sin] * reps, axis=1)
    lane = lax.broadcasted_iota(jnp.int32, x.shape, 1)
    nxt = pltpu.roll(x, n - 16, 1)
    prv = pltpu.roll(x, 16, 1)
    partner = jnp.where((lane & 16) == 0, nxt, prv)
    return x * cos + partner * sin


def _mm_body(*refs, nk, epilogue, n_extra, splits):
    x_ref, w_ref = refs[0], refs[1]
    extra = refs[2:2 + n_extra]
    o_ref = refs[2 + n_extra]
    wb_ref = refs[3 + n_extra]
    k = pl.program_id(1)
    i = pl.program_id(2)

    @pl.when(i == 0)
    def _():
        wb_ref[...] = w_ref[...].astype(BF16)

    if nk > 1:
        acc_ref = refs[4 + n_extra]

        @pl.when((pl.program_id(0) == 0) & (k == 0))
        def _():
            acc_ref[i] = jnp.zeros(acc_ref.shape[1:], F32)

    tm = x_ref.shape[0]
    sub = tm // splits
    for r in range(splits):
        rows = slice(r * sub, (r + 1) * sub)
        acc = jnp.dot(x_ref[rows, :], wb_ref[...], preferred_element_type=F32)
        if nk > 1:
            acc = jnp.where(k > 0, acc_ref[i, rows, :], 0.0) + acc
            acc_ref[i, rows, :] = acc
        epilogue(acc, rows, o_ref, *extra)


def _epi_identity(acc, rows, o_ref):
    o_ref[rows, :] = acc.astype(o_ref.dtype)


def _epi_relu2(acc, rows, o_ref):
    r = jnp.maximum(acc, 0.0)
    o_ref[rows, :] = (r * r).astype(o_ref.dtype)


def _epi_rope(acc, rows, o_ref, cos_ref, sin_ref):
    o_ref[rows, :] = _rope(acc, cos_ref[rows, :], sin_ref[rows, :]).astype(o_ref.dtype)


def _matmul(x, w, layer, out_dtype, *, epilogue=_epi_identity, rope=None, name="matmul"):
    M, K = x.shape
    N = w.shape[2]
    tk = K if K <= MM_TK_FULL else _pick_tile(K, MM_TK, LANES)
    nk = K // tk
    tm_target, tn_target = MM_TM, (MM_TN if nk > 1 else MM_TN_WHOLE_K)
    if rope is None:
        tm = _pick_tile(M, tm_target, 16)
        tn = _pick_tile(N, tn_target, LANES)
    else:
        tm = _pick_tile(rope[2], tm_target, 16)
        tn = _pick_tile(math.gcd(N, *[b for lo, hi, _ in rope[3] for b in (lo, hi)]), tn_target, LANES)
    nm = M // tm
    last = nk - 1
    in_specs = [pl.BlockSpec((tm, tk), lambda j, k, i: (i, k)),
                pl.BlockSpec((None, tk, tn), lambda j, k, i: (layer, k, j))]
    operands = [x, w]
    n_extra = 0
    if rope is not None:
        cos, sin, rows_per_batch, ranges = rope
        per = rows_per_batch // tm

        def table_set(j):
            which = TAB_IDENT
            for lo, hi, tab in ranges:
                which = jnp.where((j * tn >= lo) & (j * tn < hi), tab, which)
            return which

        tab_spec = pl.BlockSpec((None, tm, LANES), lambda j, k, i: (table_set(j), i % per, 0))
        in_specs += [tab_spec, tab_spec]
        operands += [cos, sin]
        n_extra = 2
        epilogue = _epi_rope
    scratch = [pltpu.VMEM((tk, tn), BF16)]
    if nk > 1:
        scratch.append(pltpu.VMEM((nm, tm, tn), F32))
    out_spec = pl.BlockSpec((tm, tn), lambda j, k, i: (jnp.where(k == last, i, 0), j))
    return pl.pallas_call(
        functools.partial(_mm_body, nk=nk, epilogue=epilogue, n_extra=n_extra,
                          splits=max(s for s in (MM_ROW_SPLITS, 2, 1) if tm % (16 * s) == 0)),
        grid=(N // tn, nk, nm),
        in_specs=in_specs,
        out_specs=out_spec,
        out_shape=jax.ShapeDtypeStruct((M, N), out_dtype),
        scratch_shapes=scratch,
        compiler_params=_params("parallel", "arbitrary", "arbitrary"),
        name=name,
    )(*operands)


def _mod_body(c_ref, w1_ref, w2_ref, b_ref, o_ref):
    c = c_ref[...]
    s = c * (1.0 / (1.0 + jnp.exp(-c)))
    h = jnp.dot(s.astype(BF16), w1_ref[...].astype(BF16), preferred_element_type=F32)
    o = jnp.dot(h.astype(BF16), w2_ref[...].astype(BF16), preferred_element_type=F32)
    o_ref[...] = o + b_ref[...]


def _modulation(cond, w1, w2, b):
    R, D = cond.shape
    L, _, rank = w1.shape
    N = w2.shape[2]
    tn = _pick_tile(N, 2048, LANES)
    out = pl.pallas_call(
        _mod_body,
        grid=(L, N // tn),
        in_specs=[pl.BlockSpec((R, D), lambda l, j: (0, 0)),
                  pl.BlockSpec((None, D, rank), lambda l, j: (l, 0, 0)),
                  pl.BlockSpec((None, rank, tn), lambda l, j: (l, 0, j)),
                  pl.BlockSpec((None, 1, tn), lambda l, j: (l, 0, j))],
        out_specs=pl.BlockSpec((None, R, tn), lambda l, j: (l, 0, j)),
        out_shape=jax.ShapeDtypeStruct((L, R, N), F32),
        compiler_params=_params("arbitrary", "arbitrary"),
        name="adaln_modulation",
    )(cond, w1, w2, b.reshape(L, 1, N))
    return out.reshape(L, R, N_MOD, D)


def _modulate_body(x_ref, mod_ref, h_ref, *, scale_idx, shift_idx):
    x = x_ref[0]
    scale = mod_ref[0, scale_idx:scale_idx + 1, :]
    shift = mod_ref[0, shift_idx:shift_idx + 1, :]
    h_ref[0] = (x * (1.0 + scale) + shift).astype(h_ref.dtype)


def _modulate(x, mod, n_ctx_tiles, *, scale_idx, shift_idx):
    B, T, D = x.shape
    ctx_row = B
    return pl.pallas_call(
        functools.partial(_modulate_body, scale_idx=scale_idx, shift_idx=shift_idx),
        grid=(B, T // ROW_TILE),
        in_specs=[pl.BlockSpec((1, ROW_TILE, D), lambda b, t: (b, t, 0)),
                  pl.BlockSpec((1, N_MOD, D), lambda b, t: (jnp.where(t < n_ctx_tiles, ctx_row, b), 0, 0))],
        out_specs=pl.BlockSpec((1, ROW_TILE, D), lambda b, t: (b, t, 0)),
        out_shape=jax.ShapeDtypeStruct((B, T, D), BF16),
        compiler_params=_params("parallel", "parallel"),
        name="modulate",
    )(x, mod)


def _ln_mod_body(x_ref, y_ref, modg_ref, lng_ref, lnb_ref, *rest, alpha, gate_idx, ln_idx, scale_idx,
                 shift_idx, want_h):
    if want_h:
        modn_ref, xo_ref, ho_ref = rest
    else:
        (xo_ref,) = rest
    x = x_ref[0]
    y = y_ref[0]
    gate = modg_ref[0, gate_idx:gate_idx + 1, :]
    z = alpha * x + gate * y
    mu = jnp.mean(z, axis=-1, keepdims=True)
    zc = z - mu
    var = jnp.mean(zc * zc, axis=-1, keepdims=True)
    xn = (zc * lax.rsqrt(var + LN_EPS)) * lng_ref[ln_idx:ln_idx + 1, :] + lnb_ref[ln_idx:ln_idx + 1, :]
    xo_ref[0] = xn
    if want_h:
        scale = modn_ref[0, scale_idx:scale_idx + 1, :]
        shift = modn_ref[0, shift_idx:shift_idx + 1, :]
        ho_ref[0] = (xn * (1.0 + scale) + shift).astype(ho_ref.dtype)


def _ln_mod(x, y, mod_gate, ln_g, ln_b, mod_next, n_ctx_tiles, *, layer, alpha, gate_idx, ln_idx, scale_idx=0,
            shift_idx=0):
    B, T, D = x.shape
    ctx_row = B
    want_h = mod_next is not None
    off = (T - y.shape[1]) // ROW_TILE
    assert off in (0, n_ctx_tiles)
    n_tiles = T // ROW_TILE - off
    x_spec = pl.BlockSpec((1, ROW_TILE, D), lambda b, t: (b, t + off, 0))
    row_out = pl.BlockSpec((1, ROW_TILE, D), lambda b, t: (b, t, 0))
    mod_spec = pl.BlockSpec((1, N_MOD, D), lambda b, t: (jnp.where(t + off < n_ctx_tiles, ctx_row, b), 0, 0))
    ln_spec = pl.BlockSpec((None, 2, D), lambda b, t: (layer, 0, 0))
    in_specs = [x_spec, row_out, mod_spec, ln_spec, ln_spec]
    operands = [x, y, mod_gate, ln_g, ln_b]
    out_specs = [row_out]
    out_shape = [jax.ShapeDtypeStruct((B, n_tiles * ROW_TILE, D), F32)]
    if want_h:
        in_specs.append(mod_spec)
        operands.append(mod_next)
        out_specs.append(row_out)
        out_shape.append(jax.ShapeDtypeStruct((B, n_tiles * ROW_TILE, D), BF16))
    res = pl.pallas_call(
        functools.partial(_ln_mod_body, alpha=alpha, gate_idx=gate_idx, ln_idx=ln_idx, scale_idx=scale_idx,
                          shift_idx=shift_idx, want_h=want_h),
        grid=(B, n_tiles),
        in_specs=in_specs,
        out_specs=out_specs,
        out_shape=out_shape,
        compiler_params=_params("parallel", "parallel"),
        name="resid_ln_modulate",
    )(*operands)
    return (res[0], res[1]) if want_h else (res[0], None)


def _flash(items, s_ref, scale=None, pre_scaled=False):
    c = LOG2E if pre_scaled else (None if scale is None else scale * LOG2E)

    def exp2_scaled(d):
        return jnp.exp2(d if c is None else d * c)

    def put(i):
        it = items[i]
        s = lax.dot_general(it["k"](), it["q"], (((1,), (1,)), ((), ())), preferred_element_type=F32)
        s_ref[i % FLASH_LOOKAHEAD, :it["n"], :] = s

    def pv(it, p):
        return lax.dot_general(it["v"](), p.astype(BF16), (((0,), (0,)), ((), ())), preferred_element_type=F32)

    for i in range(min(FLASH_LOOKAHEAD, len(items))):
        put(i)
    state = None
    for idx, it in enumerate(items):
        s = s_ref[idx % FLASH_LOOKAHEAD, :it["n"], :]
        if idx + FLASH_LOOKAHEAD < len(items):
            put(idx + FLASH_LOOKAHEAD)
        if pre_scaled:
            s = s * scale
            if it.get("bias") is not None:
                s = s + it["bias"]()
        cm = jnp.max(s, axis=0, keepdims=True)
        if it["first"]:
            m = cm
            p = exp2_scaled(s - m)
            l = jnp.sum(p, axis=0, keepdims=True)
            acc = pv(it, p)
        else:
            m_old, l_old, acc_old = state
            m = jnp.maximum(m_old, cm)
            alpha = exp2_scaled(m_old - m)
            p = exp2_scaled(s - m)
            l = alpha * l_old + jnp.sum(p, axis=0, keepdims=True)
            acc = alpha * acc_old + pv(it, p)
        state = (m, l, acc)
        if it["last"]:
            it["emit"]((acc / l).T)


def _score_scratch():
    return pltpu.VMEM((FLASH_LOOKAHEAD, KEY_CHUNK, TQ), F32)


def _key_chunks(n_ctx, n_keys):
    return [(0, n_ctx)] + [(lo, KEY_CHUNK) for lo in range(n_ctx, n_keys, KEY_CHUNK)]


def _problem_items(q, k_get, v_get, chunks, emit):
    return [dict(q=q, k=functools.partial(k_get, lo, n), v=functools.partial(v_get, lo, n), n=n,
                 first=ci == 0, last=ci == len(chunks) - 1, emit=emit)
            for ci, (lo, n) in enumerate(chunks)]


def _for_ctx_and_latent_tiles(qi, n_ctx_tiles, n_ctx, n_keys, run):
    @pl.when(qi < n_ctx_tiles)
    def _():
        run(_key_chunks(n_ctx, n_ctx))

    @pl.when(qi >= n_ctx_tiles)
    def _():
        run(_key_chunks(n_ctx, n_keys))


def _da_body(lam_ref, q_ref, k_ref, v_ref, g_ref, o_ref, s_ref, *, heads, n_ctx_tiles, n_ctx, lam_init):
    lp = lam_ref[...]
    lam = (jnp.exp(jnp.sum(lp[0:1] * lp[1:2], axis=-1, keepdims=True))
           - jnp.exp(jnp.sum(lp[2:3] * lp[3:4], axis=-1, keepdims=True)) + lam_init)
    lane = lax.broadcasted_iota(jnp.int32, (TQ, HEAD_V), 1)

    def run(chunks):
        items = []
        for j in range(heads):
            cols = slice(j * HEAD_V, (j + 1) * HEAD_V)
            q = q_ref[0, :, cols]
            zero = jnp.zeros_like(q)
            comps = [jnp.where(lane < DA_QK_DIM, q, zero), jnp.where(lane >= DA_QK_DIM, q, zero)]
            outs = []

            def finish(o1, cols=cols, outs=outs):
                o = outs[0] - lam * o1
                ms = jnp.mean(o * o, axis=-1, keepdims=True)
                on = (o * lax.rsqrt(ms + DA_SUB_EPS)) * g_ref[...] * (1.0 - lam_init)
                o_ref[0, :, cols] = on.astype(o_ref.dtype)

            def k_get(lo, n, cols=cols):
                return k_ref[0, lo:lo + n, cols]

            def v_get(lo, n, cols=cols):
                return v_ref[0, lo:lo + n, cols]

            items += _problem_items(comps[0], k_get, v_get, chunks, outs.append)
            items += _problem_items(comps[1], k_get, v_get, chunks, finish)
        _flash(items, s_ref)

    _for_ctx_and_latent_tiles(pl.program_id(2), n_ctx_tiles, n_ctx, k_ref.shape[1], run)


def _diff_attention(qkv, lam_params, sub_g, n_ctx, lam_init):
    B, T, W3 = qkv.shape
    H = W3 // (3 * HEAD_V)
    G = math.gcd(H, DA_HEADS_PER_STEP)
    W = G * HEAD_V
    nq = T // TQ
    return pl.pallas_call(
        functools.partial(_da_body, heads=G, n_ctx_tiles=n_ctx // TQ, n_ctx=n_ctx, lam_init=lam_init),
        grid=(B, H // G, nq),
        in_specs=[pl.BlockSpec((8, LANES), lambda b, g, i: (0, 0)),
                  pl.BlockSpec((1, TQ, W), lambda b, g, i: (b, i, g)),
                  pl.BlockSpec((1, T, W), lambda b, g, i: (b, 0, H // G + g)),
                  pl.BlockSpec((1, T, W), lambda b, g, i: (b, 0, 2 * (H // G) + g)),
                  pl.BlockSpec((1, HEAD_V), lambda b, g, i: (0, 0))],
        out_specs=pl.BlockSpec((1, TQ, W), lambda b, g, i: (b, i, g)),
        out_shape=jax.ShapeDtypeStruct((B, nq * TQ, H * HEAD_V), BF16),
        scratch_shapes=[_score_scratch()],
        compiler_params=_params("parallel", "parallel", "arbitrary"),
        name="diff_attention",
    )(lam_params, qkv, qkv, qkv, sub_g)


def _mla_norm_body(d_ref, qg_ref, kvg_ref, cos_ref, sin_ref, cq_ref, ckv_ref, kr_ref, *, q_rank, kv_rank):
    d = d_ref[0]

    def rms(x, g):
        ms = jnp.mean(x * x, axis=-1, keepdims=True)
        return (x * lax.rsqrt(ms + RMS_EPS)) * g

    cq_ref[0] = rms(d[:, :q_rank], qg_ref[...]).astype(cq_ref.dtype)
    ckv_ref[0] = rms(d[:, q_rank:q_rank + kv_rank], kvg_ref[...]).astype(ckv_ref.dtype)
    kr = d[:, q_rank + kv_rank:q_rank + kv_rank + LANES]
    kr_ref[0] = _rope(kr, cos_ref[...], sin_ref[...]).astype(kr_ref.dtype)


def _mla_norm(down, q_g, kv_g, cos, sin):
    B, T, W = down.shape
    q_rank = q_g.shape[1]
    kv_rank = kv_g.shape[1]
    row = lambda w: pl.BlockSpec((1, ROW_TILE, w), lambda b, t: (b, t, 0))
    full = lambda w: pl.BlockSpec((1, w), lambda b, t: (0, 0))
    tab = pl.BlockSpec((None, ROW_TILE, LANES), lambda b, t: (TAB_ROT, t, 0))
    return pl.pallas_call(
        functools.partial(_mla_norm_body, q_rank=q_rank, kv_rank=kv_rank),
        grid=(B, T // ROW_TILE),
        in_specs=[row(W), full(q_rank), full(kv_rank), tab, tab],
        out_specs=[row(q_rank), row(kv_rank), row(LANES)],
        out_shape=[jax.ShapeDtypeStruct((B, T, q_rank), BF16),
                   jax.ShapeDtypeStruct((B, T, kv_rank), BF16),
                   jax.ShapeDtypeStruct((B, T, LANES), BF16)],
        compiler_params=_params("parallel", "parallel"),
        name="mla_norm_rope",
    )(down, q_g, kv_g, cos, sin)


def _mla_body(qn_ref, qr_ref, kn_ref, kr_ref, v_ref, o_ref, s_ref, *, heads, n_ctx_tiles, n_ctx):
    lane = lax.broadcasted_iota(jnp.int32, (TQ, LANES), 1)

    def run(chunks):
        items = []
        for j in range(heads):
            cols = slice(j * HEAD_V, (j + 1) * HEAD_V)
            pair = slice((j // 2) * LANES, (j // 2 + 1) * LANES)
            qr = qr_ref[0, :, pair]
            mine = (lane // MLA_ROPE) == (j % 2)
            q = jnp.concatenate([qn_ref[0, :, cols], jnp.where(mine, qr, jnp.zeros_like(qr))], axis=1)

            def k_get(lo, n, cols=cols):
                return jnp.concatenate([kn_ref[0, lo:lo + n, cols], kr_ref[0, lo:lo + n, :]], axis=1)

            def v_get(lo, n, cols=cols):
                return v_ref[0, lo:lo + n, cols]

            def emit(o, cols=cols):
                o_ref[0, :, cols] = o.astype(o_ref.dtype)

            items += _problem_items(q, k_get, v_get, chunks, emit)
        _flash(items, s_ref)

    _for_ctx_and_latent_tiles(pl.program_id(2), n_ctx_tiles, n_ctx, kn_ref.shape[1], run)


def _mla_attention(q, kv, kr, n_ctx):
    B, T, Wq = q.shape
    H = Wq // (MLA_NOPE + MLA_ROPE)
    nq = T // TQ
    G = math.gcd(H, MLA_HEADS_PER_STEP)
    assert G % 2 == 0
    W = G * HEAD_V
    Wr = G * MLA_ROPE
    return pl.pallas_call(
        functools.partial(_mla_body, heads=G, n_ctx_tiles=n_ctx // TQ, n_ctx=n_ctx),
        grid=(B, H // G, nq),
        in_specs=[pl.BlockSpec((1, TQ, W), lambda b, g, i: (b, i, g)),
                  pl.BlockSpec((1, TQ, Wr), lambda b, g, i: (b, i, (H * MLA_NOPE) // Wr + g)),
                  pl.BlockSpec((1, T, W), lambda b, g, i: (b, 0, g)),
                  pl.BlockSpec((1, T, LANES), lambda b, g, i: (b, 0, 0)),
                  pl.BlockSpec((1, T, W), lambda b, g, i: (b, 0, H // G + g))],
        out_specs=pl.BlockSpec((1, TQ, W), lambda b, g, i: (b, i, g)),
        out_shape=jax.ShapeDtypeStruct((B, T, H * HEAD_V), BF16),
        scratch_shapes=[_score_scratch()],
        compiler_params=_params("parallel", "parallel", "arbitrary"),
        name="mla_attention",
    )(q, q, kv, kr, kv)


def _na_key_row_start(i, n_rows):
    return jnp.clip(NA_TILE_ROWS * i - NA_WIN_ROWS // 2, 0, n_rows - NA_KEY_ROWS)


def _na_body(q_ref, k_ref, v_ref, bias_ref, o_ref, s_ref, *, heads, n_ctx_tiles, n_ctx, n_rows):
    qi = pl.program_id(2)
    n_lat = NA_KEY_ROWS * GRID_W

    def run(latent):
        start = pl.multiple_of(n_ctx + _na_key_row_start(qi - n_ctx_tiles, n_rows) * GRID_W, GRID_W)
        items = []
        for j in range(heads):
            cols = slice(j * HEAD_V, (j + 1) * HEAD_V)

            def emit(o, cols=cols):
                o_ref[0, :, cols] = o.astype(o_ref.dtype)

            ctx_item = dict(q=q_ref[0, :, cols], n=n_ctx, first=True, last=not latent, emit=emit,
                            k=lambda cols=cols: k_ref[0, :n_ctx, cols],
                            v=lambda cols=cols: v_ref[0, :n_ctx, cols])
            items.append(ctx_item)
            if latent:
                items.append(dict(q=q_ref[0, :, cols], n=n_lat, first=False, last=True, emit=emit,
                                  k=lambda cols=cols: k_ref[0, pl.ds(start, n_lat), cols],
                                  v=lambda cols=cols: v_ref[0, pl.ds(start, n_lat), cols],
                                  bias=lambda j=j: bias_ref[j]))
        _flash(items, s_ref, HEAD_V ** -0.5, pre_scaled=True)

    @pl.when(qi < n_ctx_tiles)
    def _():
        run(False)

    @pl.when(qi >= n_ctx_tiles)
    def _():
        run(True)


def _na_bias_table(rpb, n_rows):
    H = rpb.shape[0]
    n_dr = 2 * NA_WIN_ROWS - 1
    W = GRID_W
    off_idx = np.clip(np.arange(2 * W - 1) - (W - 1) + NA_WIN_COLS - 1, 0, 2 * NA_WIN_COLS - 2)
    ext = rpb[:, :, off_idx]
    ext = jnp.broadcast_to(ext[:, :, None, :], (H, n_dr, W, 2 * W - 1))
    ext = jnp.pad(ext, ((0, 0), (0, 0), (0, 0), (0, 1))).reshape(H, n_dr, W * 2 * W)
    skew = ext[:, :, W - 1:W - 1 + W * (2 * W - 1)].reshape(H, n_dr, W, 2 * W - 1)[..., :W]
    col = np.arange(W)
    col_start = np.clip(col - NA_WIN_COLS // 2, 0, W - NA_WIN_COLS)
    col_in = (col[None, :] >= col_start[:, None]) & (col[None, :] < col_start[:, None] + NA_WIN_COLS)
    col_bias = jnp.where(col_in[None, None], skew, MASK_BIAS)
    col_bias = jnp.concatenate([col_bias, jnp.full((H, 1, W, W), MASK_BIAS, F32)], axis=1)
    n_tiles = n_rows // NA_TILE_ROWS
    tiles = np.array([0, 1, n_tiles - 1])
    ks = np.clip(NA_TILE_ROWS * tiles - NA_WIN_ROWS // 2, 0, n_rows - NA_KEY_ROWS)
    qrow = NA_TILE_ROWS * tiles[:, None] + np.arange(NA_TILE_ROWS)[None]
    krow = ks[:, None] + np.arange(NA_KEY_ROWS)[None]
    r0 = np.clip(qrow - NA_WIN_ROWS // 2, 0, n_rows - NA_WIN_ROWS)
    row_in = (krow[:, None, :] >= r0[:, :, None]) & (krow[:, None, :] < r0[:, :, None] + NA_WIN_ROWS)
    row_idx = np.where(row_in, krow[:, None, :] - qrow[:, :, None] + NA_WIN_ROWS - 1, n_dr)
    col_bias_t = jnp.swapaxes(col_bias, 2, 3)
    blocks = [jnp.concatenate([col_bias_t[:, int(row_idx[v, j, r])] for j in range(NA_TILE_ROWS)], axis=-1)
              for v in range(3) for r in range(NA_KEY_ROWS)]
    return jnp.stack(blocks, axis=1).reshape(H, 3, NA_KEY_ROWS * W, TQ)


def _na_attention(qkv, bias, n_ctx):
    B, T, W3 = qkv.shape
    H = W3 // (3 * HEAD_V)
    G = math.gcd(H, NA_HEADS_PER_STEP)
    W = G * HEAD_V
    nq = T // TQ
    n_ctx_tiles = n_ctx // TQ
    n_rows = (T - n_ctx) // GRID_W
    n_lat_tiles = nq - n_ctx_tiles

    def bias_index(b, g, i):
        t = i - n_ctx_tiles
        return (g, jnp.where(t <= 0, 0, jnp.where(t == n_lat_tiles - 1, 2, 1)), 0, 0)

    return pl.pallas_call(
        functools.partial(_na_body, heads=G, n_ctx_tiles=n_ctx_tiles, n_ctx=n_ctx, n_rows=n_rows),
        grid=(B, H // G, nq),
        in_specs=[pl.BlockSpec((1, TQ, W), lambda b, g, i: (b, i, g)),
                  pl.BlockSpec((1, T, W), lambda b, g, i: (b, 0, H // G + g)),
                  pl.BlockSpec((1, T, W), lambda b, g, i: (b, 0, 2 * (H // G) + g)),
                  pl.BlockSpec((G, None, NA_KEY_ROWS * GRID_W, TQ), bias_index)],
        out_specs=pl.BlockSpec((1, TQ, W), lambda b, g, i: (b, i, g)),
        out_shape=jax.ShapeDtypeStruct((B, T, H * HEAD_V), BF16),
        scratch_shapes=[_score_scratch()],
        compiler_params=_params("parallel", "parallel", "arbitrary"),
        name="neighbourhood_attention",
    )(qkv, qkv, qkv, bias)


def _rope_tables(n_ctx, seq, q_scale):
    pos = jnp.arange(seq)
    quarter = DA_QK_DIM // 4
    inv = ROPE_BASE ** (-jnp.arange(quarter, dtype=F32) / quarter)
    ang_r = (pos // GRID_W).astype(F32)[:, None] * inv
    ang_c = (pos % GRID_W).astype(F32)[:, None] * inv
    cos = jnp.concatenate([jnp.cos(ang_r)] * 2 + [jnp.cos(ang_c)] * 2, axis=1)
    sin = jnp.concatenate([-jnp.sin(ang_r), jnp.sin(ang_r), -jnp.sin(ang_c), jnp.sin(ang_c)], axis=1)
    cos = jnp.concatenate([jnp.ones((n_ctx, DA_QK_DIM), F32), cos], axis=0)
    sin = jnp.concatenate([jnp.zeros((n_ctx, DA_QK_DIM), F32), sin], axis=0)
    cos = jnp.concatenate([cos, cos], axis=1)
    sin = jnp.concatenate([sin, sin], axis=1)
    one, zero = jnp.ones_like(cos), jnp.zeros_like(sin)
    return (jnp.stack([cos * q_scale, cos, one, one * q_scale]),
            jnp.stack([sin * q_scale, sin, zero, zero]))


def kernel(x, c, ctx, c_ctx, ada_w1, ada_w2, ada_b, ln_g, ln_b, mlp_w1, mlp_w2, da_w_qkv, da_w_o, da_lq1,
           da_lk1, da_lq2, da_lk2, da_sub_g, mla_w_down, mla_q_g, mla_w_uq, mla_kv_g, mla_w_ukv, mla_w_o,
           na_w_qkv, na_w_o, na_rpb):
    B, S, D = x.shape
    C = ctx.shape[1]
    T = C + S
    M = B * T
    depth = ada_w1.shape[0]
    alpha = (2.0 * depth) ** 0.25
    n_ctx_tiles = C // ROW_TILE
    assert C % TQ == 0 and S % KEY_CHUNK == 0 and ROW_TILE == TQ and B + 1 <= 8
    assert C <= KEY_CHUNK and NA_KEY_ROWS * GRID_W <= KEY_CHUNK

    da_tabs = _rope_tables(C, S, (DA_QK_DIM ** -0.5) * LOG2E)
    mla_tabs = _rope_tables(C, S, ((MLA_NOPE + MLA_ROPE) ** -0.5) * LOG2E)
    xs = jnp.concatenate([ctx, x], axis=1)
    cond = jnp.concatenate([c, c_ctx[None], jnp.zeros((8 - B - 1, D), F32)], axis=0)
    all_mods = _modulation(cond, ada_w1, ada_w2, ada_b)
    mods = [all_mods[i] for i in range(depth)]

    h = _modulate(xs, mods[0], n_ctx_tiles, scale_idx=1, shift_idx=0)
    for i in range(depth):
        kind, slot = i % N_MIXERS, i // N_MIXERS
        last = i == depth - 1
        h2 = h.reshape(M, D)
        if kind == 0:
            lam_init = 0.8 - 0.6 * math.exp(-0.3 * i)
            n_heads = da_w_o.shape[1] // HEAD_V
            q_cols = n_heads * HEAD_V
            qkv = _matmul(h2, da_w_qkv, slot, BF16, name="da_qkv",
                          rope=(*da_tabs, T, [(0, q_cols, TAB_ROT_SCALED), (q_cols, 2 * q_cols, TAB_ROT)]))
            lam_params = jnp.zeros((8, LANES), F32).at[:4, :DA_QK_DIM].set(
                jnp.stack([da_lq1[slot], da_lk1[slot], da_lq2[slot], da_lk2[slot]]))
            o = _diff_attention(qkv.reshape(B, T, -1), lam_params, da_sub_g[slot][None], C, lam_init)
            w_o = da_w_o
        elif kind == 1:
            w_down = mla_w_down[slot]
            q_rank = mla_q_g.shape[1]
            kv_rank = mla_kv_g.shape[1]
            n_heads = mla_w_o.shape[1] // HEAD_V
            k_rope_w = w_down[:, q_rank + kv_rank:]
            w_down = jnp.concatenate([w_down, k_rope_w, jnp.zeros((D, LANES), F32)], axis=1)
            down = _matmul(h2, w_down[None], 0, F32, name="mla_down")
            cq, ckv, kr = _mla_norm(down.reshape(B, T, -1), mla_q_g[slot][None], mla_kv_g[slot][None], *mla_tabs)
            w_uq = mla_w_uq[slot].reshape(q_rank, n_heads, MLA_NOPE + MLA_ROPE)
            w_uq = jnp.concatenate([w_uq[:, :, :MLA_NOPE].reshape(q_rank, -1),
                                    w_uq[:, :, MLA_NOPE:].reshape(q_rank, -1)], axis=1)
            nope_cols = n_heads * MLA_NOPE
            q = _matmul(cq.reshape(M, q_rank), w_uq[None], 0, BF16,
                        rope=(*mla_tabs, T, [(0, nope_cols, TAB_IDENT_SCALED),
                                             (nope_cols, nope_cols + n_heads * MLA_ROPE, TAB_ROT_SCALED)]),
                        name="mla_uq")
            w_ukv = mla_w_ukv[slot].reshape(kv_rank, n_heads, MLA_NOPE + HEAD_V)
            w_ukv = jnp.concatenate([w_ukv[:, :, :MLA_NOPE].reshape(kv_rank, -1),
                                     w_ukv[:, :, MLA_NOPE:].reshape(kv_rank, -1)], axis=1)
            kv = _matmul(ckv.reshape(M, kv_rank), w_ukv[None], 0, BF16, name="mla_ukv")
            o = _mla_attention(q.reshape(B, T, -1), kv.reshape(B, T, -1), kr, C)
            w_o = mla_w_o
        else:
            qkv = _matmul(h2, na_w_qkv, slot, BF16, name="na_qkv")
            bias = _na_bias_table(na_rpb[slot], S // GRID_W)
            o = _na_attention(qkv.reshape(B, T, -1), bias, C)
            w_o = na_w_o
        if last and o.shape[1] == T:
            o = o[:, C:]
        rows = o.shape[1]
        y = _matmul(o.reshape(B * rows, -1), w_o, slot, F32, name="attn_out").reshape(B, rows, D)
        xs, h = _ln_mod(xs, y, mods[i], ln_g, ln_b, mods[i], n_ctx_tiles, layer=i, alpha=alpha, gate_idx=2,
                        ln_idx=0, scale_idx=4, shift_idx=3)
        u = _matmul(h.reshape(B * rows, D), mlp_w1, i, BF16, epilogue=_epi_relu2, name="mlp_up")
        y = _matmul(u, mlp_w2, i, F32, name="mlp_down").reshape(B, rows, D)
        if not last:
            xs, h = _ln_mod(xs, y, mods[i], ln_g, ln_b, mods[i + 1], n_ctx_tiles, layer=i, alpha=alpha,
                            gate_idx=5, ln_idx=1, scale_idx=1, shift_idx=0)
        else:
            xs, _ = _ln_mod(xs, y, mods[i], ln_g, ln_b, None, 0, layer=i, alpha=alpha, gate_idx=5, ln_idx=1)
    return xs
```
